```python
import math
import jax, jax.numpy as jnp
from jax import lax
import numpy as np

D_MODEL = 1024
BATCH = 2
SEQ = 16384
DEPTH = 1
DEC_BATCH = 16
DEC_SEQ = 4096
PAST_LEN = 128

A_HEADS = 8
A_HEAD_DIM = 64
A_V_DIM = 2 * A_HEAD_DIM
A_QK = 2 * A_HEADS * A_HEAD_DIM
A_WIDTH = A_HEADS * A_V_DIM
Q_BLOCK = 128
B_GROUPS = 8
B_WIDTH = 1024
B_GROUP_DIM = B_WIDTH // B_GROUPS
CHUNK = 128
REL_BUCKETS = 32
REL_MAX_DIST = 128
EPS = 1e-6

IN_SIZES = (A_QK, A_QK, A_WIDTH, A_WIDTH, B_WIDTH, B_WIDTH, B_WIDTH, D_MODEL, D_MODEL)
IN_COLS = sum(IN_SIZES)
IN_SPLITS = tuple(int(s) for s in np.cumsum(IN_SIZES)[:-1])

kernel_name = "hybrid_diffattn_gmlp_gated_encoder"


def rms_norm(x, g):
    xf = x.astype(jnp.float32)
    y = xf * lax.rsqrt(jnp.mean(xf * xf, axis=-1, keepdims=True) + EPS) * g.astype(jnp.float32)
    return y.astype(x.dtype)


def layer_norm(x, g, b):
    xf = x.astype(jnp.float32)
    mu = jnp.mean(xf, axis=-1, keepdims=True)
    xc = xf - mu
    y = xc * lax.rsqrt(jnp.mean(xc * xc, axis=-1, keepdims=True) + EPS)
    return (y * g.astype(jnp.float32) + b.astype(jnp.float32)).astype(x.dtype)


def rel_bucket(rel):
    half = REL_BUCKETS // 2
    max_exact = half // 2
    n = jnp.abs(rel)
    nf = jnp.maximum(n, 1).astype(jnp.float32)
    large = max_exact + (jnp.log(nf / max_exact) / math.log(REL_MAX_DIST / max_exact)
                         * (half - max_exact)).astype(jnp.int32)
    large = jnp.minimum(large, half - 1)
    return jnp.where(rel > 0, half, 0) + jnp.where(n < max_exact, n, large)


def diff_attention(q1, q2, k1, k2, v, lam, rel_bias):
    B, S, H, d = q1.shape
    nblk = S // Q_BLOCK
    scale = d ** -0.5
    kpos = jnp.arange(S)
    table = rel_bias.astype(jnp.float32)

    def to_blocks(t):
        return t.reshape(B, nblk, Q_BLOCK, H, d).swapaxes(0, 1)

    def one_block(args):
        i, a1, a2 = args
        qpos = i * Q_BLOCK + jnp.arange(Q_BLOCK)
        bias = table[rel_bucket(kpos[None, :] - qpos[:, None])].transpose(2, 0, 1)
        s1 = jnp.einsum('bqhd,bkhd->bhqk', a1, k1, preferred_element_type=jnp.float32) * scale + bias
        s2 = jnp.einsum('bqhd,bkhd->bhqk', a2, k2, preferred_element_type=jnp.float32) * scale + bias
        p = jax.nn.softmax(s1, axis=-1) - lam * jax.nn.softmax(s2, axis=-1)
        return jnp.einsum('bhqk,bkhe->bqhe', p.astype(v.dtype), v)

    o = lax.map(one_block, (jnp.arange(nblk), to_blocks(q1), to_blocks(q2)))
    return o.swapaxes(0, 1).reshape(B, S, H, v.shape[-1])


def spatial_gate(u, v, ln_g, ln_b, w_s, b_s):
    B, S, _ = v.shape
    vn = layer_norm(v, ln_g, ln_b).reshape(B, S // CHUNK, CHUNK, B_GROUPS, B_GROUP_DIM)
    s = jnp.einsum('gpq,bcqgk->bcpgk', w_s, vn) + b_s.T[None, None, :, :, None]
    return u * s.reshape(B, S, B_WIDTH)


def encoder_layer(x, layer_idx, g_pre, w_in, lambda_q1, lambda_k1, lambda_q2, lambda_k2, subln_g,
                  w_pa, ln_g, ln_b, w_s, b_s, w_pb, w_o, g_post, rel_bias):
    B, S, _ = x.shape
    lambda_init = 0.8 - 0.6 * math.exp(-0.3 * layer_idx)
    h = rms_norm(x, g_pre)
    z = h @ w_in
    q, k, v_a, gate_a, u_b, v_b, gate_b, m_a, m_b = jnp.split(z, IN_SPLITS, axis=-1)
    q = q.reshape(B, S, 2, A_HEADS, A_HEAD_DIM)
    k = k.reshape(B, S, 2, A_HEADS, A_HEAD_DIM)
    v_a = v_a.reshape(B, S, A_HEADS, A_V_DIM)
    lam = (jnp.exp(jnp.sum(lambda_q1.astype(jnp.float32) * lambda_k1.astype(jnp.float32)))
           - jnp.exp(jnp.sum(lambda_q2.astype(jnp.float32) * lambda_k2.astype(jnp.float32)))
           + lambda_init)
    o = diff_attention(q[:, :, 0], q[:, :, 1], k[:, :, 0], k[:, :, 1], v_a, lam, rel_bias)
    o = rms_norm(o, subln_g) * (1.0 - lambda_init)
    y_a = (o.reshape(B, S, A_WIDTH) * jax.nn.silu(gate_a)) @ w_pa
    y_b = (spatial_gate(u_b, v_b, ln_g, ln_b, w_s, b_s) * jax.nn.silu(gate_b)) @ w_pb
    merged = jax.nn.sigmoid(m_a) * y_a + jax.nn.sigmoid(m_b) * y_b
    out = merged @ w_o
    return x + rms_norm(out, g_post)


def setup_inputs(seed: int = 0) -> dict:
    key = jax.random.key(seed)
    ks = jax.random.split(key, 20)
    f32 = jnp.float32
    nrm = lambda k, shape, s: jax.random.normal(k, shape, f32) * s
    return {
        "x_prompt": nrm(ks[0], (BATCH, SEQ, D_MODEL), 1.0),
        "x_sample": nrm(ks[1], (DEC_BATCH, DEC_SEQ, D_MODEL), 1.0),
        "g_pre": 1.0 + nrm(ks[2], (DEPTH, D_MODEL), 0.02),
        "w_in": nrm(ks[3], (DEPTH, D_MODEL, IN_COLS), D_MODEL ** -0.5),
        "lambda_q1": nrm(ks[4], (DEPTH, A_HEAD_DIM), 0.1),
        "lambda_k1": nrm(ks[5], (DEPTH, A_HEAD_DIM), 0.1),
        "lambda_q2": nrm(ks[6], (DEPTH, A_HEAD_DIM), 0.1),
        "lambda_k2": nrm(ks[7], (DEPTH, A_HEAD_DIM), 0.1),
        "subln_g": 1.0 + nrm(ks[8], (DEPTH, A_V_DIM), 0.02),
        "w_pa": nrm(ks[9], (DEPTH, A_WIDTH, D_MODEL), A_WIDTH ** -0.5),
        "ln_g": 1.0 + nrm(ks[10], (DEPTH, B_WIDTH), 0.02),
        "ln_b": nrm(ks[11], (DEPTH, B_WIDTH), 0.02),
        "w_s": nrm(ks[12], (DEPTH, B_GROUPS, CHUNK, CHUNK), CHUNK ** -0.5),
        "b_s": 1.0 + nrm(ks[13], (DEPTH, B_GROUPS, CHUNK), 0.1),
        "w_pb": nrm(ks[14], (DEPTH, B_WIDTH, D_MODEL), B_WIDTH ** -0.5),
        "w_o": nrm(ks[15], (DEPTH, D_MODEL, D_MODEL), D_MODEL ** -0.5),
        "g_post": 1.0 + nrm(ks[16], (DEPTH, D_MODEL), 0.02),
        "rel_bias": nrm(ks[17], (REL_BUCKETS, A_HEADS), 0.2),
    }


def reference(x_prompt, x_sample, g_pre, w_in, lambda_q1, lambda_k1, lambda_q2, lambda_k2, subln_g,
              w_pa, ln_g, ln_b, w_s, b_s, w_pb, w_o, g_post, rel_bias):
    y_prompt = x_prompt
    y_sample = x_sample
    for l in range(DEPTH):
        args = (g_pre[l], w_in[l], lambda_q1[l], lambda_k1[l], lambda_q2[l], lambda_k2[l], subln_g[l],
                w_pa[l], ln_g[l], ln_b[l], w_s[l], b_s[l], w_pb[l], w_o[l], g_post[l], rel_bias)
        y_prompt = encoder_layer(y_prompt, l, *args)
        y_sample = encoder_layer(y_sample, l, *args)
    return (y_prompt, y_sample)
```

```python
import functools
import math

import jax
import jax.numpy as jnp
import numpy as np
from jax import lax
from jax.experimental import pallas as pl
from jax.experimental.pallas import tpu as pltpu

D_MODEL = 1024
HEADS = 8
HEAD_DIM = 64
V_DIM = 2 * HEAD_DIM
GROUPS = 8
GROUP_DIM = 128
CHUNK = 128
REL_BUCKETS = 32
REL_MAX_DIST = 128
EPS = 1e-6
N_COL_BLOCKS = 9
LOG2E = math.log2(math.e)
LANES = 128
VMEM_LIMIT_BYTES = 56 * 1024 * 1024
NEG_BIG = -1e30


def _rel_bucket(rel):
    half = REL_BUCKETS // 2
    max_exact = half // 2
    n = jnp.abs(rel)
    nf = jnp.maximum(n, 1).astype(jnp.float32)
    large = max_exact + (jnp.log(nf / max_exact) / math.log(REL_MAX_DIST / max_exact)
                         * (half - max_exact)).astype(jnp.int32)
    large = jnp.minimum(large, half - 1)
    return jnp.where(rel > 0, half, 0) + jnp.where(n < max_exact, n, large)


FAR_LEFT_BUCKET = REL_BUCKETS // 2 - 1
FAR_RIGHT_BUCKET = REL_BUCKETS - 1


def _tiles(seq):
    tm = min(512, seq)
    tq = min(256, seq)
    tk = min(1024, seq)
    return tm, tq, tk


def _bias_strip_kernel(bucket_ref, table_ref, out_ref):
    h = pl.program_id(0)
    bucket = bucket_ref[0]
    acc = jnp.zeros(bucket.shape, jnp.float32)
    for b in range(REL_BUCKETS):
        acc = jnp.where(bucket == b, table_ref[b, h], acc)
    out_ref[0, 0] = acc * LOG2E


def _bias_strips(rel_bias, tq, tk):
    n_chunks = (2 * tk + tq) // LANES
    col = jnp.arange(n_chunks * LANES, dtype=jnp.int32).reshape(n_chunks, 1, LANES)
    row = jnp.arange(tq, dtype=jnp.int32).reshape(1, tq, 1)
    bucket = _rel_bucket(col - row - tk).astype(jnp.int32)
    return pl.pallas_call(
        _bias_strip_kernel,
        grid=(HEADS, n_chunks),
        in_specs=[
            pl.BlockSpec((1, tq, LANES), lambda h, c: (c, 0, 0)),
            pl.BlockSpec(memory_space=pltpu.SMEM),
        ],
        out_specs=pl.BlockSpec((1, 1, tq, LANES), lambda h, c: (h, c, 0, 0)),
        out_shape=jax.ShapeDtypeStruct((HEADS, n_chunks, tq, LANES), jnp.float32),
        name="bias_strips",
    )(bucket, rel_bias.astype(jnp.float32))


def _silu(x):
    return x * jax.nn.sigmoid(x)


def _proj_kernel(x_ref, gpre_ref, win_ref, lng_ref, lnb_ref, ws_ref, bs_ref, wpb_ref,
                 q_ref, k_ref, v_ref, ga_ref, sma_ref, ybg_ref, t_scr):
    tm = x_ref.shape[1]
    x = x_ref[0]
    h = x * lax.rsqrt(jnp.mean(x * x, axis=-1, keepdims=True) + EPS) * gpre_ref[...]
    h = h.astype(jnp.bfloat16)

    def zcol(c):
        return jnp.dot(h, win_ref[:, c * D_MODEL:(c + 1) * D_MODEL],
                       preferred_element_type=jnp.float32)

    def store_heads(ref, z):
        zb = z.astype(jnp.bfloat16)
        for hd in range(HEADS):
            ref[0, hd] = zb[:, hd * V_DIM:(hd + 1) * V_DIM]

    store_heads(q_ref, zcol(0) * (LOG2E * HEAD_DIM ** -0.5))
    store_heads(k_ref, zcol(1))
    store_heads(v_ref, zcol(2))
    ga_ref[0] = _silu(zcol(3)).astype(jnp.bfloat16)

    vb = zcol(5)
    mu = jnp.mean(vb, axis=-1, keepdims=True)
    vc = vb - mu
    vn = vc * lax.rsqrt(jnp.mean(vc * vc, axis=-1, keepdims=True) + EPS)
    vn = (vn * lng_ref[...] + lnb_ref[...]).astype(jnp.bfloat16)
    ug = zcol(4) * _silu(zcol(6))
    for c in range(tm // CHUNK):
        rows = slice(c * CHUNK, (c + 1) * CHUNK)
        for g in range(GROUPS):
            cols = slice(g * GROUP_DIM, (g + 1) * GROUP_DIM)
            s = jnp.dot(ws_ref[g], vn[rows, cols], preferred_element_type=jnp.float32) + bs_ref[g]
            t_scr[rows, cols] = (ug[rows, cols] * s).astype(jnp.bfloat16)
    y_b = jnp.dot(t_scr[...], wpb_ref[...], preferred_element_type=jnp.float32)

    sma_ref[0] = jax.nn.sigmoid(zcol(7))
    ybg_ref[0] = jax.nn.sigmoid(zcol(8)) * y_b


def _resident(shape):
    return pl.BlockSpec(shape, lambda *_: (0,) * len(shape), pipeline_mode=pl.Buffered(1))


def _project(x, g_pre, w_in_b, ln_g, ln_b, w_s_b, bs_b, w_pb_b, tm):
    batch, seq, _ = x.shape
    head_shape = jax.ShapeDtypeStruct((batch, HEADS, seq, V_DIM), jnp.bfloat16)
    head_spec = pl.BlockSpec((1, HEADS, tm, V_DIM), lambda b, i: (b, 0, i, 0))
    tok_spec = pl.BlockSpec((1, tm, D_MODEL), lambda b, i: (b, i, 0))
    return pl.pallas_call(
        _proj_kernel,
        grid=(batch, seq // tm),
        in_specs=[
            tok_spec,
            _resident((1, D_MODEL)),
            _resident((D_MODEL, N_COL_BLOCKS * D_MODEL)),
            _resident((1, D_MODEL)),
            _resident((1, D_MODEL)),
            _resident((GROUPS, CHUNK, CHUNK)),
            _resident((GROUPS, CHUNK, GROUP_DIM)),
            _resident((D_MODEL, D_MODEL)),
        ],
        out_specs=[head_spec, head_spec, head_spec, tok_spec, tok_spec, tok_spec],
        out_shape=[
            head_shape, head_shape, head_shape,
            jax.ShapeDtypeStruct((batch, seq, D_MODEL), jnp.bfloat16),
            jax.ShapeDtypeStruct((batch, seq, D_MODEL), jnp.float32),
            jax.ShapeDtypeStruct((batch, seq, D_MODEL), jnp.float32),
        ],
        scratch_shapes=[pltpu.VMEM((tm, D_MODEL), jnp.bfloat16)],
        compiler_params=pltpu.CompilerParams(
            dimension_semantics=("parallel", "parallel"), vmem_limit_bytes=VMEM_LIMIT_BYTES),
        name="proj_gmlp",
    )(x, g_pre, w_in_b, ln_g, ln_b, w_s_b, bs_b, w_pb_b)


def _attn_kernel(far_ref, lq1_ref, lk1_ref, lq2_ref, lk2_ref, q_ref, k_ref, v_ref, strip_ref,
                 ga_ref, subg_ref, o_ref, m_scr, l_scr, acc_scr, *, tq, tk, lambda_init):
    hd = pl.program_id(1)
    qi = pl.program_id(2)
    seq = k_ref.shape[2]
    nk = seq // tk
    ratio = tk // tq

    q = q_ref[0, 0]
    lane = lax.broadcasted_iota(jnp.int32, q.shape, 1)
    zero = jnp.zeros_like(q)
    qmaps = (jnp.where(lane < HEAD_DIM, q, zero), jnp.where(lane >= HEAD_DIM, q, zero))

    m_scr[...] = jnp.full(m_scr.shape, NEG_BIG, jnp.float32)
    l_scr[...] = jnp.zeros(l_scr.shape, jnp.float32)
    acc_scr[...] = jnp.zeros(acc_scr.shape, jnp.float32)

    def tile(kt, bias_fn):
        k0 = pl.multiple_of(kt * tk, tk)
        kblk = k_ref[0, 0, pl.ds(k0, tk), :]
        vblk = v_ref[0, 0, pl.ds(k0, tk), :]
        for mp in range(2):
            s = lax.dot_general(qmaps[mp], kblk, (((1,), (1,)), ((), ())),
                                preferred_element_type=jnp.float32)
            s = bias_fn(s)
            m_old = m_scr[mp]
            m_new = jnp.maximum(m_old, jnp.max(s, axis=1, keepdims=True))
            alpha = jnp.exp2(m_old - m_new)
            p = jnp.exp2(s - m_new)
            l_scr[mp] = alpha * l_scr[mp] + jnp.sum(p, axis=1, keepdims=True)
            acc_scr[mp] = alpha * acc_scr[mp] + jnp.dot(
                p.astype(jnp.bfloat16), vblk, preferred_element_type=jnp.float32)
            m_scr[mp] = m_new

    def far_tile(bias_const):
        def body(kt, carry):
            tile(kt, lambda s: s + bias_const)
            return carry
        return body

    def band_tile(kt, carry):
        c0 = (kt * ratio - qi + ratio) * (tq // LANES)

        def add_bias(s):
            bias = jnp.concatenate(
                [strip_ref[0, c0 + c] for c in range(tk // LANES)], axis=1)
            return s + bias
        tile(kt, add_bias)
        return carry

    band_lo = jnp.maximum((qi + ratio - 1) // ratio - 1, 0)
    band_hi = jnp.minimum((qi + 1) // ratio, nk - 1)
    lax.fori_loop(0, band_lo, far_tile(far_ref[0, hd]), 0)
    lax.fori_loop(band_lo, band_hi + 1, band_tile, 0)
    lax.fori_loop(band_hi + 1, nk, far_tile(far_ref[1, hd]), 0)

    lam = (jnp.exp(jnp.sum(lq1_ref[...] * lk1_ref[...], axis=-1, keepdims=True))
           - jnp.exp(jnp.sum(lq2_ref[...] * lk2_ref[...], axis=-1, keepdims=True)) + lambda_init)
    o = acc_scr[0] / l_scr[0] - lam * (acc_scr[1] / l_scr[1])
    o = o * lax.rsqrt(jnp.mean(o * o, axis=-1, keepdims=True) + EPS) * subg_ref[...]
    o = o * (1.0 - lambda_init)
    o_ref[0] = (o * ga_ref[0].astype(jnp.float32)).astype(jnp.bfloat16)


def _attention(far_bias, lam_vecs, q12, k12, v, strips, ga, subln_g, lambda_init, tq, tk):
    batch, _, seq, _ = q12.shape
    n_chunks = strips.shape[1]
    smem = pl.BlockSpec(memory_space=pltpu.SMEM)
    vec = pl.BlockSpec((1, HEAD_DIM), lambda b, h, i: (0, 0))
    kv_spec = pl.BlockSpec((1, 1, seq, V_DIM), lambda b, h, i: (b, h, 0, 0))
    return pl.pallas_call(
        functools.partial(_attn_kernel, tq=tq, tk=tk, lambda_init=lambda_init),
        grid=(batch, HEADS, seq // tq),
        in_specs=[
            smem, vec, vec, vec, vec,
            pl.BlockSpec((1, 1, tq, V_DIM), lambda b, h, i: (b, h, i, 0)),
            kv_spec, kv_spec,
            pl.BlockSpec((1, n_chunks, tq, LANES), lambda b, h, i: (h, 0, 0, 0)),
            pl.BlockSpec((1, tq, V_DIM), lambda b, h, i: (b, i, h)),
            pl.BlockSpec((1, V_DIM), lambda b, h, i: (0, 0)),
        ],
        out_specs=pl.BlockSpec((1, tq, V_DIM), lambda b, h, i: (b, i, h)),
        out_shape=jax.ShapeDtypeStruct((batch, seq, HEADS * V_DIM), jnp.bfloat16),
        scratch_shapes=[
            pltpu.VMEM((2, tq, 1), jnp.float32),
            pltpu.VMEM((2, tq, 1), jnp.float32),
            pltpu.VMEM((2, tq, V_DIM), jnp.float32),
        ],
        compiler_params=pltpu.CompilerParams(
            dimension_semantics=("parallel", "parallel", "arbitrary"),
            vmem_limit_bytes=VMEM_LIMIT_BYTES),
        name="diff_attention",
    )(far_bias, *lam_vecs, q12, k12, v, strips, ga, subln_g)


def _out_kernel(x_ref, og_ref, sma_ref, ybg_ref, wpa_ref, wo_ref, gpost_ref, y_ref):
    y_a = jnp.dot(og_ref[0], wpa_ref[...], preferred_element_type=jnp.float32)
    merged = sma_ref[0] * y_a + ybg_ref[0]
    out = jnp.dot(merged.astype(jnp.bfloat16), wo_ref[...], preferred_element_type=jnp.float32)
    normed = out * lax.rsqrt(jnp.mean(out * out, axis=-1, keepdims=True) + EPS) * gpost_ref[...]
    y_ref[0] = x_ref[0] + normed


def _output(x, og, sma, ybg, w_pa_b, w_o_b, g_post, tm):
    batch, seq, _ = x.shape
    tok_spec = pl.BlockSpec((1, tm, D_MODEL), lambda b, i: (b, i, 0))
    return pl.pallas_call(
        _out_kernel,
        grid=(batch, seq // tm),
        in_specs=[tok_spec, tok_spec, tok_spec, tok_spec,
                  _resident((D_MODEL, D_MODEL)), _resident((D_MODEL, D_MODEL)),
                  _resident((1, D_MODEL))],
        out_specs=tok_spec,
        out_shape=jax.ShapeDtypeStruct(x.shape, x.dtype),
        compiler_params=pltpu.CompilerParams(
            dimension_semantics=("parallel", "parallel"), vmem_limit_bytes=VMEM_LIMIT_BYTES),
        name="merge_out",
    )(x, og, sma, ybg, w_pa_b, w_o_b, g_post)


def _head_major_columns():
    idx = np.arange(HEADS * V_DIM)
    hd, mp, d = idx // V_DIM, (idx % V_DIM) // HEAD_DIM, idx % HEAD_DIM
    return mp * (HEADS * HEAD_DIM) + hd * HEAD_DIM + d


def _encoder_layer(x, layer_idx, g_pre, w_in, lambda_q1, lambda_k1, lambda_q2, lambda_k2, subln_g,
                   w_pa, ln_g, ln_b, w_s, b_s, w_pb, w_o, g_post, rel_bias):
    seq = x.shape[1]
    tm, tq, tk = _tiles(seq)
    assert seq % tm == 0 and seq % tk == 0 and tk % tq == 0 and tq % LANES == 0 and tm % CHUNK == 0
    lambda_init = 0.8 - 0.6 * math.exp(-0.3 * layer_idx)
    bf16 = jnp.bfloat16
    row = lambda a: a.reshape(1, -1).astype(jnp.float32)

    perm = _head_major_columns()
    cols = np.concatenate([perm, D_MODEL + perm, np.arange(2 * D_MODEL, N_COL_BLOCKS * D_MODEL)])
    w_in_b = w_in[:, cols].astype(bf16)
    bs_b = jnp.broadcast_to(b_s.astype(jnp.float32)[:, :, None], (GROUPS, CHUNK, GROUP_DIM))

    q12, k12, v, ga, sma, ybg = _project(
        x, row(g_pre), w_in_b, row(ln_g), row(ln_b), w_s.astype(bf16), bs_b, w_pb.astype(bf16), tm)

    strips = _bias_strips(rel_bias, tq, tk)
    far_bias = jnp.stack([rel_bias[FAR_LEFT_BUCKET], rel_bias[FAR_RIGHT_BUCKET]]).astype(jnp.float32) * LOG2E
    lam_vecs = [row(a) for a in (lambda_q1, lambda_k1, lambda_q2, lambda_k2)]
    og = _attention(far_bias, lam_vecs, q12, k12, v, strips, ga, row(subln_g), lambda_init, tq, tk)

    return _output(x, og, sma, ybg, w_pa.astype(bf16), w_o.astype(bf16), row(g_post), tm)


def kernel(x_prompt, x_sample, g_pre, w_in, lambda_q1, lambda_k1, lambda_q2, lambda_k2, subln_g,
           w_pa, ln_g, ln_b, w_s, b_s, w_pb, w_o, g_post, rel_bias):
    y_prompt, y_sample = x_prompt, x_sample
    for l in range(g_pre.shape[0]):
        args = (g_pre[l], w_in[l], lambda_q1[l], lambda_k1[l], lambda_q2[l], lambda_k2[l], subln_g[l],
                w_pa[l], ln_g[l], ln_b[l], w_s[l], b_s[l], w_pb[l], w_o[l], g_post[l], rel_bias)
        y_prompt = _encoder_layer(y_prompt, l, *args)
        y_sample = _encoder_layer(y_sample, l, *args)
    return (y_prompt, y_sample)
```

```python
import functools
import math

import jax
import jax.numpy as jnp
import numpy as np
from jax import lax
from jax.experimental import pallas as pl
from jax.experimental.pallas import tpu as pltpu

D_MODEL = 1024
HEADS = 8
HEAD_DIM = 64
V_DIM = 2 * HEAD_DIM
GROUPS = 8
GROUP_DIM = 128
CHUNK = 128
REL_BUCKETS = 32
REL_MAX_DIST = 128
EPS = 1e-6
N_COL_BLOCKS = 9
LOG2E = math.log2(math.e)
LANES = 128
BF16_SUBLANES = 16
PV_CHUNK = 256
V_ROWS = V_DIM + BF16_SUBLANES
VMEM_LIMIT_BYTES = 56 * 1024 * 1024
NEG_BIG = -1e30


def _rel_bucket(rel):
    half = REL_BUCKETS // 2
    max_exact = half // 2
    n = jnp.abs(rel)
    nf = jnp.maximum(n, 1).astype(jnp.float32)
    large = max_exact + (jnp.log(nf / max_exact) / math.log(REL_MAX_DIST / max_exact)
                         * (half - max_exact)).astype(jnp.int32)
    large = jnp.minimum(large, half - 1)
    return jnp.where(rel > 0, half, 0) + jnp.where(n < max_exact, n, large)


def _tiles(seq):
    tm = min(512, seq)
    tq = min(256, seq)
    tk = min(1024, seq)
    return tm, tq, tk


def _bias_strip_kernel(bucket_ref, table_ref, out_ref):
    h = pl.program_id(0)
    bucket = bucket_ref[...]
    acc = jnp.zeros(bucket.shape, jnp.float32)
    for b in range(REL_BUCKETS):
        acc = jnp.where(bucket == b, table_ref[b, h], acc)
    out_ref[0] = acc * LOG2E


def _bias_strips(rel_bias, tq, tk):
    n_rows = 2 * tk + 3 * tq
    rows_per_step = tq
    key = jnp.arange(n_rows, dtype=jnp.int32).reshape(n_rows, 1)
    qry = jnp.arange(tq, dtype=jnp.int32).reshape(1, tq)
    bucket = _rel_bucket(key - qry - tk - tq).astype(jnp.int32)
    return pl.pallas_call(
        _bias_strip_kernel,
        grid=(HEADS, n_rows // rows_per_step),
        in_specs=[
            pl.BlockSpec((rows_per_step, tq), lambda h, c: (c, 0)),
            pl.BlockSpec(memory_space=pltpu.SMEM),
        ],
        out_specs=pl.BlockSpec((1, rows_per_step, tq), lambda h, c: (h, c, 0)),
        out_shape=jax.ShapeDtypeStruct((HEADS, n_rows, tq), jnp.float32),
        name="bias_strips",
    )(bucket, rel_bias.astype(jnp.float32))


def _silu(x):
    return x * jax.nn.sigmoid(x)


def _proj_kernel(x_ref, gpre_ref, win_ref, lng_ref, lnb_ref, ws_ref, bs_ref, wpb_ref,
                 q_ref, k_ref, vt_ref, ga_ref, sma_ref, ybg_ref, t_scr):
    tm = x_ref.shape[1]
    x = x_ref[0]
    h = x * lax.rsqrt(jnp.mean(x * x, axis=-1, keepdims=True) + EPS) * gpre_ref[...]
    h = h.astype(jnp.bfloat16)

    def zcol(c):
        return jnp.dot(h, win_ref[:, c * D_MODEL:(c + 1) * D_MODEL],
                       preferred_element_type=jnp.float32)

    def store_heads(ref, z):
        zb = z.astype(jnp.bfloat16)
        for hd in range(HEADS):
            ref[0, hd] = zb[:, hd * V_DIM:(hd + 1) * V_DIM]

    store_heads(q_ref, zcol(0) * (LOG2E * HEAD_DIM ** -0.5))
    store_heads(k_ref, zcol(1))
    zv = zcol(2)
    ones = jnp.ones((BF16_SUBLANES, tm), jnp.bfloat16)
    for hd in range(HEADS):
        vt_ref[0, hd, 0, :V_DIM, :] = zv[:, hd * V_DIM:(hd + 1) * V_DIM].T.astype(jnp.bfloat16)
        vt_ref[0, hd, 0, V_DIM:, :] = ones
    ga_ref[0] = _silu(zcol(3)).astype(jnp.bfloat16)

    vb = zcol(5)
    mu = jnp.mean(vb, axis=-1, keepdims=True)
    vc = vb - mu
    vn = vc * lax.rsqrt(jnp.mean(vc * vc, axis=-1, keepdims=True) + EPS)
    vn = (vn * lng_ref[...] + lnb_ref[...]).astype(jnp.bfloat16)
    ug = zcol(4) * _silu(zcol(6))
    for c in range(tm // CHUNK):
        rows = slice(c * CHUNK, (c + 1) * CHUNK)
        for g in range(GROUPS):
            cols = slice(g * GROUP_DIM, (g + 1) * GROUP_DIM)
            s = jnp.dot(ws_ref[g], vn[rows, cols], preferred_element_type=jnp.float32) + bs_ref[g]
            t_scr[rows, cols] = (ug[rows, cols] * s).astype(jnp.bfloat16)
    y_b = jnp.dot(t_scr[...], wpb_ref[...], preferred_element_type=jnp.float32)

    sma_ref[0] = jax.nn.sigmoid(zcol(7))
    ybg_ref[0] = jax.nn.sigmoid(zcol(8)) * y_b


def _resident(shape):
    return pl.BlockSpec(shape, lambda *_: (0,) * len(shape), pipeline_mode=pl.Buffered(1))


def _project(x, g_pre, w_in_b, ln_g, ln_b, w_s_b, bs_b, w_pb_b, tm, tk):
    batch, seq, _ = x.shape
    per_k = tk // tm
    head_shape = jax.ShapeDtypeStruct((batch, HEADS, seq, V_DIM), jnp.bfloat16)
    head_spec = pl.BlockSpec((1, HEADS, tm, V_DIM), lambda b, i: (b, 0, i, 0))
    tok_spec = pl.BlockSpec((1, tm, D_MODEL), lambda b, i: (b, i, 0))
    return pl.pallas_call(
        _proj_kernel,
        grid=(batch, seq // tm),
        in_specs=[
            tok_spec,
            _resident((1, D_MODEL)),
            _resident((D_MODEL, N_COL_BLOCKS * D_MODEL)),
            _resident((1, D_MODEL)),
            _resident((1, D_MODEL)),
            _resident((GROUPS, CHUNK, CHUNK)),
            _resident((GROUPS, CHUNK, GROUP_DIM)),
            _resident((D_MODEL, D_MODEL)),
        ],
        out_specs=[
            head_spec, head_spec,
            pl.BlockSpec((1, HEADS, 1, V_ROWS, tm), lambda b, i: (b, 0, i // per_k, 0, i % per_k)),
            tok_spec, tok_spec, tok_spec],
        out_shape=[
            head_shape, head_shape,
            jax.ShapeDtypeStruct((batch, HEADS, seq // tk, V_ROWS, tk), jnp.bfloat16),
            jax.ShapeDtypeStruct((batch, seq, D_MODEL), jnp.bfloat16),
            jax.ShapeDtypeStruct((batch, seq, D_MODEL), jnp.float32),
            jax.ShapeDtypeStruct((batch, seq, D_MODEL), jnp.float32),
        ],
        scratch_shapes=[pltpu.VMEM((tm, D_MODEL), jnp.bfloat16)],
        compiler_params=pltpu.CompilerParams(
            dimension_semantics=("parallel", "parallel"), vmem_limit_bytes=VMEM_LIMIT_BYTES),
        name="proj_gmlp",
    )(x, g_pre, w_in_b, ln_g, ln_b, w_s_b, bs_b, w_pb_b)


def _attn_kernel(lq1_ref, lk1_ref, lq2_ref, lk2_ref, q_ref, k_ref, vt_ref, strip_ref,
                 ga_ref, subg_ref, o_ref, s_scr, m_scr, ao_scr, acc_scr, *, tq, tk, lambda_init):
    qi = pl.program_id(2)
    nk = vt_ref.shape[2]
    ratio = tk // tq

    q = q_ref[0, 0]
    lane = lax.broadcasted_iota(jnp.int32, q.shape, 1)
    zero = jnp.zeros_like(q)
    qmaps = (jnp.where(lane < HEAD_DIM, q, zero), jnp.where(lane >= HEAD_DIM, q, zero))

    m_scr[...] = jnp.full(m_scr.shape, NEG_BIG, jnp.float32)
    acc_scr[...] = jnp.zeros(acc_scr.shape, jnp.float32)

    def scores(kt, buf):
        k0 = pl.multiple_of(kt * tk, tk)
        kblk = k_ref[0, 0, pl.ds(k0, tk), :]
        r0 = jnp.clip((kt * ratio - qi + ratio + 1) * tq, 0, tk + 3 * tq)
        bias = strip_ref[0, pl.ds(pl.multiple_of(r0, tq), tk), :]
        for mp in range(2):
            s = lax.dot_general(kblk, qmaps[mp], (((1,), (1,)), ((), ())),
                                preferred_element_type=jnp.float32) + bias
            s_scr[buf, mp] = s
            m_old = m_scr[mp]
            m_new = jnp.maximum(m_old, jnp.max(s, axis=0, keepdims=True))
            m_scr[mp] = m_new
            ao_scr[buf, mp, 0] = jnp.exp2(m_old - m_new)
            ao_scr[buf, mp, 1] = m_new

    def accumulate(kt, buf):
        vt = vt_ref.at[0, 0, kt]
        for mp in range(2):
            off = ao_scr[buf, mp, 1]
            pv = None
            for c in range(tk // PV_CHUNK):
                keys = slice(c * PV_CHUNK, (c + 1) * PV_CHUNK)
                p = jnp.exp2(s_scr[buf, mp, keys, :] - off).astype(jnp.bfloat16)
                part = jnp.dot(vt[:, keys], p, preferred_element_type=jnp.float32)
                pv = part if pv is None else pv + part
            acc_scr[mp] = ao_scr[buf, mp, 0] * acc_scr[mp] + pv

    scores(0, 0)

    def tile_pair(j, carry):
        t = 2 * j
        scores(t + 1, 1)
        accumulate(t, 0)
        scores(t + 2, 0)
        accumulate(t + 1, 1)
        return carry

    lax.fori_loop(0, nk // 2 - 1, tile_pair, 0)
    scores(nk - 1, 1)
    accumulate(nk - 2, 0)
    accumulate(nk - 1, 1)

    lam = (jnp.exp(jnp.sum(lq1_ref[...] * lk1_ref[...], axis=-1, keepdims=True))
           - jnp.exp(jnp.sum(lq2_ref[...] * lk2_ref[...], axis=-1, keepdims=True)) + lambda_init)
    inv1 = 1.0 / acc_scr[0, V_DIM:V_DIM + 1, :]
    inv2 = lam / acc_scr[1, V_DIM:V_DIM + 1, :]
    o = acc_scr[0, :V_DIM, :] * inv1 - acc_scr[1, :V_DIM, :] * inv2
    o = o * lax.rsqrt(jnp.mean(o * o, axis=0, keepdims=True) + EPS) * subg_ref[...]
    o = o * (1.0 - lambda_init)
    o_ref[0] = (o.T * ga_ref[0].astype(jnp.float32)).astype(jnp.bfloat16)


def _attention(lam_vecs, q12, k12, vt, strips, ga, subln_g, lambda_init, tq, tk):
    batch, _, seq, _ = q12.shape
    nk = seq // tk
    assert nk >= 2 and nk % 2 == 0
    vec = pl.BlockSpec((1, HEAD_DIM), lambda b, h, i: (0, 0))
    return pl.pallas_call(
        functools.partial(_attn_kernel, tq=tq, tk=tk, lambda_init=lambda_init),
        grid=(batch, HEADS, seq // tq),
        in_specs=[
            vec, vec, vec, vec,
            pl.BlockSpec((1, 1, tq, V_DIM), lambda b, h, i: (b, h, i, 0)),
            pl.BlockSpec((1, 1, seq, V_DIM), lambda b, h, i: (b, h, 0, 0)),
            pl.BlockSpec((1, 1, nk, V_ROWS, tk), lambda b, h, i: (b, h, 0, 0, 0)),
            pl.BlockSpec((1, strips.shape[1], tq), lambda b, h, i: (h, 0, 0)),
            pl.BlockSpec((1, tq, V_DIM), lambda b, h, i: (b, i, h)),
            pl.BlockSpec((V_DIM, 1), lambda b, h, i: (0, 0)),
        ],
        out_specs=pl.BlockSpec((1, tq, V_DIM), lambda b, h, i: (b, i, h)),
        out_shape=jax.ShapeDtypeStruct((batch, seq, HEADS * V_DIM), jnp.bfloat16),
        scratch_shapes=[
            pltpu.VMEM((2, 2, tk, tq), jnp.float32),
            pltpu.VMEM((2, 1, tq), jnp.float32),
            pltpu.VMEM((2, 2, 2, 1, tq), jnp.float32),
            pltpu.VMEM((2, V_ROWS, tq), jnp.float32),
        ],
        compiler_params=pltpu.CompilerParams(
            dimension_semantics=("parallel", "parallel", "arbitrary"),
            vmem_limit_bytes=VMEM_LIMIT_BYTES),
        name="diff_attention",
    )(*lam_vecs, q12, k12, vt, strips, ga, subln_g)


def _out_kernel(x_ref, og_ref, sma_ref, ybg_ref, wpa_ref, wo_ref, gpost_ref, y_ref):
    y_a = jnp.dot(og_ref[0], wpa_ref[...], preferred_element_type=jnp.float32)
    merged = sma_ref[0] * y_a + ybg_ref[0]
    out = jnp.dot(merged.astype(jnp.bfloat16), wo_ref[...], preferred_element_type=jnp.float32)
    normed = out * lax.rsqrt(jnp.mean(out * out, axis=-1, keepdims=True) + EPS) * gpost_ref[...]
    y_ref[0] = x_ref[0] + normed


def _output(x, og, sma, ybg, w_pa_b, w_o_b, g_post, tm):
    batch, seq, _ = x.shape
    tok_spec = pl.BlockSpec((1, tm, D_MODEL), lambda b, i: (b, i, 0))
    return pl.pallas_call(
        _out_kernel,
        grid=(batch, seq // tm),
        in_specs=[tok_spec, tok_spec, tok_spec, tok_spec,
                  _resident((D_MODEL, D_MODEL)), _resident((D_MODEL, D_MODEL)),
                  _resident((1, D_MODEL))],
        out_specs=tok_spec,
        out_shape=jax.ShapeDtypeStruct(x.shape, x.dtype),
        compiler_params=pltpu.CompilerParams(
            dimension_semantics=("parallel", "parallel"), vmem_limit_bytes=VMEM_LIMIT_BYTES),
        name="merge_out",
    )(x, og, sma, ybg, w_pa_b, w_o_b, g_post)


def _head_major_columns():
    idx = np.arange(HEADS * V_DIM)
    hd, mp, d = idx // V_DIM, (idx % V_DIM) // HEAD_DIM, idx % HEAD_DIM
    return mp * (HEADS * HEAD_DIM) + hd * HEAD_DIM + d


def _encoder_layer(x, layer_idx, g_pre, w_in, lambda_q1, lambda_k1, lambda_q2, lambda_k2, subln_g,
                   w_pa, ln_g, ln_b, w_s, b_s, w_pb, w_o, g_post, rel_bias):
    seq = x.shape[1]
    tm, tq, tk = _tiles(seq)
    assert seq % tk == 0 and tk % tm == 0 and tk % tq == 0 and tq % LANES == 0 and tm % CHUNK == 0
    lambda_init = 0.8 - 0.6 * math.exp(-0.3 * layer_idx)
    bf16 = jnp.bfloat16
    row = lambda a: a.reshape(1, -1).astype(jnp.float32)

    perm = _head_major_columns()
    cols = np.concatenate([perm, D_MODEL + perm, np.arange(2 * D_MODEL, N_COL_BLOCKS * D_MODEL)])
    w_in_b = w_in[:, cols].astype(bf16)
    bs_b = jnp.broadcast_to(b_s.astype(jnp.float32)[:, :, None], (GROUPS, CHUNK, GROUP_DIM))

    q12, k12, vt, ga, sma, ybg = _project(
        x, row(g_pre), w_in_b, row(ln_g), row(ln_b), w_s.astype(bf16), bs_b, w_pb.astype(bf16), tm, tk)

    strips = _bias_strips(rel_bias, tq, tk)
    lam_vecs = [row(a) for a in (lambda_q1, lambda_k1, lambda_q2, lambda_k2)]
    subg_col = subln_g.reshape(V_DIM, 1).astype(jnp.float32)
    og = _attention(lam_vecs, q12, k12, vt, strips, ga, subg_col, lambda_init, tq, tk)

    return _output(x, og, sma, ybg, w_pa.astype(bf16), w_o.astype(bf16), row(g_post), tm)


def kernel(x_prompt, x_sample, g_pre, w_in, lambda_q1, lambda_k1, lambda_q2, lambda_k2, subln_g,
           w_pa, ln_g, ln_b, w_s, b_s, w_pb, w_o, g_post, rel_bias):
    y_prompt, y_sample = x_prompt, x_sample
    for l in range(g_pre.shape[0]):
        args = (g_pre[l], w_in[l], lambda_q1[l], lambda_k1[l], lambda_q2[l], lambda_k2[l], subln_g[l],
                w_pa[l], ln_g[l], ln_b[l], w_s[l], b_s[l], w_pb[l], w_o[l], g_post[l], rel_bias)
        y_prompt = _encoder_layer(y_prompt, l, *args)
        y_sample = _encoder_layer(y_sample, l, *args)
    return (y_prompt, y_sample)
```

```python
import functools
import math

import jax
import jax.numpy as jnp
import numpy as np
from jax import lax
from jax.experimental import pallas as pl
from jax.experimental.pallas import tpu as pltpu

D_MODEL = 1024
HEADS = 8
HEAD_DIM = 64
V_DIM = 2 * HEAD_DIM
GROUPS = 8
GROUP_DIM = 128
CHUNK = 128
REL_BUCKETS = 32
REL_MAX_DIST = 128
EPS = 1e-6
N_COL_BLOCKS = 9
LOG2E = math.log2(math.e)
LANES = 128
SUBLANES = 8
BF16_SUBLANES = 16
PV_CHUNK = 256
V_ROWS = V_DIM + BF16_SUBLANES
VMEM_LIMIT_BYTES = 56 * 1024 * 1024
NEG_BIG = -1e30
BOUND_SLACK = 1.0 + 2.0 ** -6
MIN_DENOMINATOR = 2.0 ** -64
MAX_DENOMINATOR = 2.0 ** 64


def _rel_bucket(rel):
    half = REL_BUCKETS // 2
    max_exact = half // 2
    n = jnp.abs(rel)
    nf = jnp.maximum(n, 1).astype(jnp.float32)
    large = max_exact + (jnp.log(nf / max_exact) / math.log(REL_MAX_DIST / max_exact)
                         * (half - max_exact)).astype(jnp.int32)
    large = jnp.minimum(large, half - 1)
    return jnp.where(rel > 0, half, 0) + jnp.where(n < max_exact, n, large)


def _tiles(seq):
    tm = min(512, seq)
    tq = min(256, seq)
    tk = min(1024, seq)
    group = min(4, seq // tk)
    return tm, tq, tk, group


def _bias_strip_kernel(bucket_ref, table_ref, out_ref):
    h = pl.program_id(0)
    bucket = bucket_ref[...]
    acc = jnp.zeros(bucket.shape, jnp.float32)
    for b in range(REL_BUCKETS):
        acc = jnp.where(bucket == b, table_ref[b, h], acc)
    out_ref[0] = acc * LOG2E


def _bias_strips(rel_bias, tq, tk):
    n_rows = 2 * tk + 3 * tq
    rows_per_step = tq
    key = jnp.arange(n_rows, dtype=jnp.int32).reshape(n_rows, 1)
    qry = jnp.arange(tq, dtype=jnp.int32).reshape(1, tq)
    bucket = _rel_bucket(key - qry - tk - tq).astype(jnp.int32)
    return pl.pallas_call(
        _bias_strip_kernel,
        grid=(HEADS, n_rows // rows_per_step),
        in_specs=[
            pl.BlockSpec((rows_per_step, tq), lambda h, c: (c, 0)),
            pl.BlockSpec(memory_space=pltpu.SMEM),
        ],
        out_specs=pl.BlockSpec((1, rows_per_step, tq), lambda h, c: (h, c, 0)),
        out_shape=jax.ShapeDtypeStruct((HEADS, n_rows, tq), jnp.float32),
        name="bias_strips",
    )(bucket, rel_bias.astype(jnp.float32))


def _silu(x):
    return x * jax.nn.sigmoid(x)


def _proj_kernel(x_ref, gpre_ref, win_ref, lng_ref, lnb_ref, ws_ref, bs_ref, wpb_ref,
                 q_ref, k_ref, vt_ref, ga_ref, sma_ref, ybg_ref, t_scr):
    tm = x_ref.shape[1]
    x = x_ref[0]
    h = x * lax.rsqrt(jnp.mean(x * x, axis=-1, keepdims=True) + EPS) * gpre_ref[...]
    h = h.astype(jnp.bfloat16)

    def zcol(c):
        return jnp.dot(h, win_ref[:, c * D_MODEL:(c + 1) * D_MODEL],
                       preferred_element_type=jnp.float32)

    def store_heads(ref, z):
        zb = z.astype(jnp.bfloat16)
        for hd in range(HEADS):
            ref[0, hd] = zb[:, hd * V_DIM:(hd + 1) * V_DIM]

    store_heads(q_ref, zcol(0) * (LOG2E * HEAD_DIM ** -0.5))
    store_heads(k_ref, zcol(1))
    zv = zcol(2)
    ones = jnp.ones((BF16_SUBLANES, tm), jnp.bfloat16)
    for hd in range(HEADS):
        vt_ref[0, hd, 0, :V_DIM, :] = zv[:, hd * V_DIM:(hd + 1) * V_DIM].T.astype(jnp.bfloat16)
        vt_ref[0, hd, 0, V_DIM:, :] = ones
    ga_ref[0] = _silu(zcol(3)).astype(jnp.bfloat16)

    vb = zcol(5)
    mu = jnp.mean(vb, axis=-1, keepdims=True)
    vc = vb - mu
    vn = vc * lax.rsqrt(jnp.mean(vc * vc, axis=-1, keepdims=True) + EPS)
    vn = (vn * lng_ref[...] + lnb_ref[...]).astype(jnp.bfloat16)
    ug = zcol(4) * _silu(zcol(6))
    for c in range(tm // CHUNK):
        rows = slice(c * CHUNK, (c + 1) * CHUNK)
        for g in range(GROUPS):
            cols = slice(g * GROUP_DIM, (g + 1) * GROUP_DIM)
            s = jnp.dot(ws_ref[g], vn[rows, cols], preferred_element_type=jnp.float32) + bs_ref[g]
            t_scr[rows, cols] = (ug[rows, cols] * s).astype(jnp.bfloat16)
    y_b = jnp.dot(t_scr[...], wpb_ref[...], preferred_element_type=jnp.float32)

    sma_ref[0] = jax.nn.sigmoid(zcol(7))
    ybg_ref[0] = jax.nn.sigmoid(zcol(8)) * y_b


def _resident(shape):
    return pl.BlockSpec(shape, lambda *_: (0,) * len(shape), pipeline_mode=pl.Buffered(1))


def _project(x, g_pre, w_in_b, ln_g, ln_b, w_s_b, bs_b, w_pb_b, tm, tk):
    batch, seq, _ = x.shape
    per_k = tk // tm
    head_shape = jax.ShapeDtypeStruct((batch, HEADS, seq, V_DIM), jnp.bfloat16)
    head_spec = pl.BlockSpec((1, HEADS, tm, V_DIM), lambda b, i: (b, 0, i, 0))
    tok_spec = pl.BlockSpec((1, tm, D_MODEL), lambda b, i: (b, i, 0))
    return pl.pallas_call(
        _proj_kernel,
        grid=(batch, seq // tm),
        in_specs=[
            tok_spec,
            _resident((1, D_MODEL)),
            _resident((D_MODEL, N_COL_BLOCKS * D_MODEL)),
            _resident((1, D_MODEL)),
            _resident((1, D_MODEL)),
            _resident((GROUPS, CHUNK, CHUNK)),
            _resident((GROUPS, CHUNK, GROUP_DIM)),
            _resident((D_MODEL, D_MODEL)),
        ],
        out_specs=[
            head_spec, head_spec,
            pl.BlockSpec((1, HEADS, 1, V_ROWS, tm), lambda b, i: (b, 0, i // per_k, 0, i % per_k)),
            tok_spec, tok_spec, tok_spec],
        out_shape=[
            head_shape, head_shape,
            jax.ShapeDtypeStruct((batch, HEADS, seq // tk, V_ROWS, tk), jnp.bfloat16),
            jax.ShapeDtypeStruct((batch, seq, D_MODEL), jnp.bfloat16),
            jax.ShapeDtypeStruct((batch, seq, D_MODEL), jnp.float32),
            jax.ShapeDtypeStruct((batch, seq, D_MODEL), jnp.float32),
        ],
        scratch_shapes=[pltpu.VMEM((tm, D_MODEL), jnp.bfloat16)],
        compiler_params=pltpu.CompilerParams(
            dimension_semantics=("parallel", "parallel"), vmem_limit_bytes=VMEM_LIMIT_BYTES),
        name="proj_gmlp",
    )(x, g_pre, w_in_b, ln_g, ln_b, w_s_b, bs_b, w_pb_b)


def _scores(lhs, rhs):
    return lax.dot_general(lhs, rhs, (((1,), (1,)), ((), ())), preferred_element_type=jnp.float32)


def _attn_kernel(lq1_ref, lk1_ref, lq2_ref, lk2_ref, q_ref, k_ref, vt_ref, strip_ref,
                 ga_ref, subg_ref, o_ref, acc_scr, info_scr, s_scr, m_scr,
                 *, tq, tk, group, lambda_init):
    qi = pl.program_id(2)
    nk = vt_ref.shape[2]
    ratio = tk // tq

    q = q_ref[0, 0]
    lane = lax.broadcasted_iota(jnp.int32, q.shape, 1)
    zero = jnp.zeros_like(q)
    qmaps = (jnp.where(lane < HEAD_DIM, q, zero), jnp.where(lane >= HEAD_DIM, q, zero))

    sel_row = lax.broadcasted_iota(jnp.int32, (BF16_SUBLANES, V_DIM), 0)
    sel_lane = lax.broadcasted_iota(jnp.int32, (BF16_SUBLANES, V_DIM), 1)
    half_sel = jnp.where(sel_row == sel_lane // HEAD_DIM, 1.0, 0.0).astype(jnp.bfloat16)

    def half_norms_sq(x):
        xf = x.astype(jnp.float32)
        return _scores(half_sel, (xf * xf).astype(jnp.bfloat16))

    @pl.when(qi == 0)
    def _():
        def key_tile(kt, best):
            k0 = pl.multiple_of(kt * tk, tk)
            return jnp.maximum(best, half_norms_sq(k_ref[0, 0, pl.ds(k0, tk), :]))
        best = lax.fori_loop(0, nk, key_tile, jnp.zeros((BF16_SUBLANES, tk), jnp.float32))
        kmax_sq = jnp.max(best, axis=1, keepdims=True)
        bias_max = jnp.max(jnp.max(strip_ref[0], axis=0, keepdims=True), axis=1, keepdims=True)
        info_row = lax.broadcasted_iota(jnp.int32, info_scr.shape, 0)
        info_scr[...] = jnp.where(info_row == 2, bias_max, jnp.broadcast_to(kmax_sq, info_scr.shape))

    q_sq = half_norms_sq(q)
    bias_max = info_scr[2:3, 0:1]
    ref = [jnp.sqrt(q_sq[mp:mp + 1, :] * info_scr[mp:mp + 1, 0:1]) * BOUND_SLACK + bias_max
           for mp in range(2)]

    def strip_row(kt):
        return pl.multiple_of(jnp.clip((kt * ratio - qi + ratio + 1) * tq, 0, tk + 3 * tq), tq)

    def bound_tile(kt):
        k0 = pl.multiple_of(kt * tk, tk)
        kblk = k_ref[0, 0, pl.ds(k0, tk), :]
        r0 = strip_row(kt)
        ss = [_scores(kblk, qmaps[mp]) for mp in range(2)]
        pv = [None, None]
        for c in range(tk // PV_CHUNK):
            keys = slice(c * PV_CHUNK, (c + 1) * PV_CHUNK)
            bias = strip_ref[0, pl.ds(r0 + c * PV_CHUNK, PV_CHUNK), :]
            for mp in range(2):
                p = jnp.exp2(ss[mp][keys] + bias - ref[mp]).astype(jnp.bfloat16)
                part = jnp.dot(vt_ref[0, 0, kt, :, keys], p, preferred_element_type=jnp.float32)
                pv[mp] = part if pv[mp] is None else pv[mp] + part
        for mp in range(2):
            acc_scr[mp] += pv[mp]

    def bound_tiles(g, carry):
        for t in range(group):
            bound_tile(g * group + t)
        return carry

    acc_scr[...] = jnp.zeros(acc_scr.shape, jnp.float32)
    lax.fori_loop(0, nk // group, bound_tiles, 0)

    def running_max_tile(kt, carry):
        k0 = pl.multiple_of(kt * tk, tk)
        kblk = k_ref[0, 0, pl.ds(k0, tk), :]
        bias = strip_ref[0, pl.ds(strip_row(kt), tk), :]
        for mp in range(2):
            s = _scores(kblk, qmaps[mp]) + bias
            s_scr[...] = s
            m_old = m_scr[mp]
            m_new = jnp.maximum(m_old, jnp.max(s, axis=0, keepdims=True))
            m_scr[mp] = m_new
            p = jnp.exp2(s_scr[...] - m_new).astype(jnp.bfloat16)
            acc_scr[mp] = jnp.exp2(m_old - m_new) * acc_scr[mp] + jnp.dot(
                vt_ref[0, 0, kt], p, preferred_element_type=jnp.float32)
        return carry

    denom = acc_scr[:, V_DIM:V_DIM + 1, :]
    trusted = (denom >= MIN_DENOMINATOR) & (denom <= MAX_DENOMINATOR)
    untrusted = jnp.sum(jnp.where(trusted, 0.0, 1.0))

    @pl.when(untrusted > 0.0)
    def _():
        m_scr[...] = jnp.full(m_scr.shape, NEG_BIG, jnp.float32)
        acc_scr[...] = jnp.zeros(acc_scr.shape, jnp.float32)
        lax.fori_loop(0, nk, running_max_tile, 0)

    lam = (jnp.exp(jnp.sum(lq1_ref[...] * lk1_ref[...], axis=-1, keepdims=True))
           - jnp.exp(jnp.sum(lq2_ref[...] * lk2_ref[...], axis=-1, keepdims=True)) + lambda_init)
    inv1 = 1.0 / acc_scr[0, V_DIM:V_DIM + 1, :]
    inv2 = lam / acc_scr[1, V_DIM:V_DIM + 1, :]
    o = acc_scr[0, :V_DIM, :] * inv1 - acc_scr[1, :V_DIM, :] * inv2
    o = o * lax.rsqrt(jnp.mean(o * o, axis=0, keepdims=True) + EPS) * subg_ref[...]
    o = o * (1.0 - lambda_init)
    o_ref[0] = (o.T * ga_ref[0].astype(jnp.float32)).astype(jnp.bfloat16)


def _attention(lam_vecs, q12, k12, vt, strips, ga, subln_g, lambda_init, tq, tk, group):
    batch, _, seq, _ = q12.shape
    nk = seq // tk
    assert nk % group == 0
    vec = pl.BlockSpec((1, HEAD_DIM), lambda b, h, i: (0, 0))
    return pl.pallas_call(
        functools.partial(_attn_kernel, tq=tq, tk=tk, group=group, lambda_init=lambda_init),
        grid=(batch, HEADS, seq // tq),
        in_specs=[
            vec, vec, vec, vec,
            pl.BlockSpec((1, 1, tq, V_DIM), lambda b, h, i: (b, h, i, 0)),
            pl.BlockSpec((1, 1, seq, V_DIM), lambda b, h, i: (b, h, 0, 0)),
            pl.BlockSpec((1, 1, nk, V_ROWS, tk), lambda b, h, i: (b, h, 0, 0, 0)),
            pl.BlockSpec((1, strips.shape[1], tq), lambda b, h, i: (h, 0, 0)),
            pl.BlockSpec((1, tq, V_DIM), lambda b, h, i: (b, i, h)),
            pl.BlockSpec((V_DIM, 1), lambda b, h, i: (0, 0)),
        ],
        out_specs=pl.BlockSpec((1, tq, V_DIM), lambda b, h, i: (b, i, h)),
        out_shape=jax.ShapeDtypeStruct((batch, seq, HEADS * V_DIM), jnp.bfloat16),
        scratch_shapes=[
            pltpu.VMEM((2, V_ROWS, tq), jnp.float32),
            pltpu.VMEM((BF16_SUBLANES, LANES), jnp.float32),
            pltpu.VMEM((tk, tq), jnp.float32),
            pltpu.VMEM((2, 1, tq), jnp.float32),
        ],
        compiler_params=pltpu.CompilerParams(
            dimension_semantics=("parallel", "parallel", "arbitrary"),
            vmem_limit_bytes=VMEM_LIMIT_BYTES),
        name="diff_attention",
    )(*lam_vecs, q12, k12, vt, strips, ga, subln_g)


def _out_kernel(x_ref, og_ref, sma_ref, ybg_ref, wpa_ref, wo_ref, gpost_ref, y_ref):
    y_a = jnp.dot(og_ref[0], wpa_ref[...], preferred_element_type=jnp.float32)
    merged = sma_ref[0] * y_a + ybg_ref[0]
    out = jnp.dot(merged.astype(jnp.bfloat16), wo_ref[...], preferred_element_type=jnp.float32)
    normed = out * lax.rsqrt(jnp.mean(out * out, axis=-1, keepdims=True) + EPS) * gpost_ref[...]
    y_ref[0] = x_ref[0] + normed


def _output(x, og, sma, ybg, w_pa_b, w_o_b, g_post, tm):
    batch, seq, _ = x.shape
    tok_spec = pl.BlockSpec((1, tm, D_MODEL), lambda b, i: (b, i, 0))
    return pl.pallas_call(
        _out_kernel,
        grid=(batch, seq // tm),
        in_specs=[tok_spec, tok_spec, tok_spec, tok_spec,
                  _resident((D_MODEL, D_MODEL)), _resident((D_MODEL, D_MODEL)),
                  _resident((1, D_MODEL))],
        out_specs=tok_spec,
        out_shape=jax.ShapeDtypeStruct(x.shape, x.dtype),
        compiler_params=pltpu.CompilerParams(
            dimension_semantics=("parallel", "parallel"), vmem_limit_bytes=VMEM_LIMIT_BYTES),
        name="merge_out",
    )(x, og, sma, ybg, w_pa_b, w_o_b, g_post)


def _head_major_columns():
    idx = np.arange(HEADS * V_DIM)
    hd, mp, d = idx // V_DIM, (idx % V_DIM) // HEAD_DIM, idx % HEAD_DIM
    return mp * (HEADS * HEAD_DIM) + hd * HEAD_DIM + d


def _encoder_layer(x, layer_idx, g_pre, w_in, lambda_q1, lambda_k1, lambda_q2, lambda_k2, subln_g,
                   w_pa, ln_g, ln_b, w_s, b_s, w_pb, w_o, g_post, rel_bias):
    seq = x.shape[1]
    tm, tq, tk, group = _tiles(seq)
    assert seq % tk == 0 and tk % tm == 0 and tk % tq == 0 and tq % LANES == 0 and tm % CHUNK == 0
    lambda_init = 0.8 - 0.6 * math.exp(-0.3 * layer_idx)
    bf16 = jnp.bfloat16
    row = lambda a: a.reshape(1, -1).astype(jnp.float32)

    perm = _head_major_columns()
    cols = np.concatenate([perm, D_MODEL + perm, np.arange(2 * D_MODEL, N_COL_BLOCKS * D_MODEL)])
    w_in_b = w_in[:, cols].astype(bf16)
    bs_b = jnp.broadcast_to(b_s.astype(jnp.float32)[:, :, None], (GROUPS, CHUNK, GROUP_DIM))

    q12, k12, vt, ga, sma, ybg = _project(
        x, row(g_pre), w_in_b, row(ln_g), row(ln_b), w_s.astype(bf16), bs_b, w_pb.astype(bf16), tm, tk)

    strips = _bias_strips(rel_bias, tq, tk)
    lam_vecs = [row(a) for a in (lambda_q1, lambda_k1, lambda_q2, lambda_k2)]
    subg_col = subln_g.reshape(V_DIM, 1).astype(jnp.float32)
    og = _attention(lam_vecs, q12, k12, vt, strips, ga, subg_col, lambda_init, tq, tk, group)

    return _output(x, og, sma, ybg, w_pa.astype(bf16), w_o.astype(bf16), row(g_post), tm)


def kernel(x_prompt, x_sample, g_pre, w_in, lambda_q1, lambda_k1, lambda_q2, lambda_k2, subln_g,
           w_pa, ln_g, ln_b, w_s, b_s, w_pb, w_o, g_post, rel_bias):
    y_prompt, y_sample = x_prompt, x_sample
    for l in range(g_pre.shape[0]):
        args = (g_pre[l], w_in[l], lambda_q1[l], lambda_k1[l], lambda_q2[l], lambda_k2[l], subln_g[l],
                w_pa[l], ln_g[l], ln_b[l], w_s[l], b_s[l], w_pb[l], w_o[l], g_post[l], rel_bias)
        y_prompt = _encoder_layer(y_prompt, l, *args)
        y_sample = _encoder_layer(y_sample, l, *args)
    return (y_prompt, y_sample)
```

```python
import functools
import math

import jax
import jax.numpy as jnp
import numpy as np
from jax import lax
from jax.experimental import pallas as pl
from jax.experimental.pallas import tpu as pltpu

D_MODEL = 1024
HEADS = 8
HEAD_DIM = 64
V_DIM = 2 * HEAD_DIM
GROUPS = 8
GROUP_DIM = 128
CHUNK = 128
REL_BUCKETS = 32
REL_MAX_DIST = 128
EPS = 1e-6
N_COL_BLOCKS = 9
LOG2E = math.log2(math.e)
LANES = 128
SUBLANES = 8
BF16_SUBLANES = 16
PV_CHUNK = 256
V_ROWS = V_DIM + BF16_SUBLANES
VMEM_LIMIT_BYTES = 56 * 1024 * 1024
NEG_BIG = -1e30
BOUND_SLACK = 1.0 + 2.0 ** -6
MIN_DENOMINATOR = 2.0 ** -64
MAX_DENOMINATOR = 2.0 ** 64


def _rel_bucket(rel):
    half = REL_BUCKETS // 2
    max_exact = half // 2
    n = jnp.abs(rel)
    nf = jnp.maximum(n, 1).astype(jnp.float32)
    large = max_exact + (jnp.log(nf / max_exact) / math.log(REL_MAX_DIST / max_exact)
                         * (half - max_exact)).astype(jnp.int32)
    large = jnp.minimum(large, half - 1)
    return jnp.where(rel > 0, half, 0) + jnp.where(n < max_exact, n, large)


def _tiles(seq):
    tm = min(512, seq)
    tk = min(1024, seq)
    nk = seq // tk
    group = min(8, nk)
    tq = min(512 if nk <= 4 else 256, seq)
    return tm, tq, tk, group


def _bias_strip_kernel(bucket_ref, table_ref, out_ref):
    h = pl.program_id(0)
    bucket = bucket_ref[...]
    acc = jnp.zeros(bucket.shape, jnp.float32)
    for b in range(REL_BUCKETS):
        acc = jnp.where(bucket == b, table_ref[b, h], acc)
    out_ref[0] = acc * LOG2E


def _bias_strips(rel_bias, tq, tk):
    n_rows = 2 * tk + 3 * tq
    rows_per_step = tq
    key = jnp.arange(n_rows, dtype=jnp.int32).reshape(n_rows, 1)
    qry = jnp.arange(tq, dtype=jnp.int32).reshape(1, tq)
    bucket = _rel_bucket(key - qry - tk - tq).astype(jnp.int32)
    return pl.pallas_call(
        _bias_strip_kernel,
        grid=(HEADS, n_rows // rows_per_step),
        in_specs=[
            pl.BlockSpec((rows_per_step, tq), lambda h, c: (c, 0)),
            pl.BlockSpec(memory_space=pltpu.SMEM),
        ],
        out_specs=pl.BlockSpec((1, rows_per_step, tq), lambda h, c: (h, c, 0)),
        out_shape=jax.ShapeDtypeStruct((HEADS, n_rows, tq), jnp.float32),
        name="bias_strips",
    )(bucket, rel_bias.astype(jnp.float32))


def _silu(x):
    return x * jax.nn.sigmoid(x)


def _proj_kernel(x_ref, gpre_ref, win_ref, lng_ref, lnb_ref, ws_ref, bs_ref, wpb_ref,
                 q_ref, k_ref, vt_ref, ga_ref, sma_ref, ybg_ref, t_scr):
    tm = x_ref.shape[1]
    x = x_ref[0]
    h = x * lax.rsqrt(jnp.mean(x * x, axis=-1, keepdims=True) + EPS) * gpre_ref[...]
    h = h.astype(jnp.bfloat16)

    def zcol(c):
        return jnp.dot(h, win_ref[:, c * D_MODEL:(c + 1) * D_MODEL],
                       preferred_element_type=jnp.float32)

    def store_heads(ref, z):
        zb = z.astype(jnp.bfloat16)
        for hd in range(HEADS):
            ref[0, hd] = zb[:, hd * V_DIM:(hd + 1) * V_DIM]

    store_heads(q_ref, zcol(0) * (LOG2E * HEAD_DIM ** -0.5))
    store_heads(k_ref, zcol(1))
    zv = zcol(2)
    ones = jnp.ones((BF16_SUBLANES, tm), jnp.bfloat16)
    for hd in range(HEADS):
        vt_ref[0, hd, 0, :V_DIM, :] = zv[:, hd * V_DIM:(hd + 1) * V_DIM].T.astype(jnp.bfloat16)
        vt_ref[0, hd, 0, V_DIM:, :] = ones
    ga_ref[0] = _silu(zcol(3)).astype(jnp.bfloat16)

    vb = zcol(5)
    mu = jnp.mean(vb, axis=-1, keepdims=True)
    vc = vb - mu
    vn = vc * lax.rsqrt(jnp.mean(vc * vc, axis=-1, keepdims=True) + EPS)
    vn = (vn * lng_ref[...] + lnb_ref[...]).astype(jnp.bfloat16)
    ug = zcol(4) * _silu(zcol(6))
    for c in range(tm // CHUNK):
        rows = slice(c * CHUNK, (c + 1) * CHUNK)
        for g in range(GROUPS):
            cols = slice(g * GROUP_DIM, (g + 1) * GROUP_DIM)
            s = jnp.dot(ws_ref[g], vn[rows, cols], preferred_element_type=jnp.float32) + bs_ref[g]
            t_scr[rows, cols] = (ug[rows, cols] * s).astype(jnp.bfloat16)
    y_b = jnp.dot(t_scr[...], wpb_ref[...], preferred_element_type=jnp.float32)

    sma_ref[0] = jax.nn.sigmoid(zcol(7))
    ybg_ref[0] = jax.nn.sigmoid(zcol(8)) * y_b


def _resident(shape):
    return pl.BlockSpec(shape, lambda *_: (0,) * len(shape), pipeline_mode=pl.Buffered(1))


def _project(x, g_pre, w_in_b, ln_g, ln_b, w_s_b, bs_b, w_pb_b, tm, tk):
    batch, seq, _ = x.shape
    per_k = tk // tm
    head_shape = jax.ShapeDtypeStruct((batch, HEADS, seq, V_DIM), jnp.bfloat16)
    head_spec = pl.BlockSpec((1, HEADS, tm, V_DIM), lambda b, i: (b, 0, i, 0))
    tok_spec = pl.BlockSpec((1, tm, D_MODEL), lambda b, i: (b, i, 0))
    return pl.pallas_call(
        _proj_kernel,
        grid=(batch, seq // tm),
        in_specs=[
            tok_spec,
            _resident((1, D_MODEL)),
            _resident((D_MODEL, N_COL_BLOCKS * D_MODEL)),
            _resident((1, D_MODEL)),
            _resident((1, D_MODEL)),
            _resident((GROUPS, CHUNK, CHUNK)),
            _resident((GROUPS, CHUNK, GROUP_DIM)),
            _resident((D_MODEL, D_MODEL)),
        ],
        out_specs=[
            head_spec, head_spec,
            pl.BlockSpec((1, HEADS, 1, V_ROWS, tm), lambda b, i: (b, 0, i // per_k, 0, i % per_k)),
            tok_spec, tok_spec, tok_spec],
        out_shape=[
            head_shape, head_shape,
            jax.ShapeDtypeStruct((batch, HEADS, seq // tk, V_ROWS, tk), jnp.bfloat16),
            jax.ShapeDtypeStruct((batch, seq, D_MODEL), jnp.bfloat16),
            jax.ShapeDtypeStruct((batch, seq, D_MODEL), jnp.float32),
            jax.ShapeDtypeStruct((batch, seq, D_MODEL), jnp.float32),
        ],
        scratch_shapes=[pltpu.VMEM((tm, D_MODEL), jnp.bfloat16)],
        compiler_params=pltpu.CompilerParams(
            dimension_semantics=("parallel", "parallel"), vmem_limit_bytes=VMEM_LIMIT_BYTES),
        name="proj_gmlp",
    )(x, g_pre, w_in_b, ln_g, ln_b, w_s_b, bs_b, w_pb_b)


def _scores(lhs, rhs):
    return lax.dot_general(lhs, rhs, (((1,), (1,)), ((), ())), preferred_element_type=jnp.float32)


def _attn_kernel(lq1_ref, lk1_ref, lq2_ref, lk2_ref, q_ref, k_ref, vt_ref, strip_ref,
                 ga_ref, subg_ref, o_ref, acc_scr, info_scr, s_scr, m_scr,
                 *, tq, tk, group, lambda_init):
    qi = pl.program_id(2)
    nk = vt_ref.shape[2]
    ratio = tk // tq

    q = q_ref[0, 0]
    lane = lax.broadcasted_iota(jnp.int32, q.shape, 1)
    zero = jnp.zeros_like(q)
    qmaps = (jnp.where(lane < HEAD_DIM, q, zero), jnp.where(lane >= HEAD_DIM, q, zero))

    sel_row = lax.broadcasted_iota(jnp.int32, (BF16_SUBLANES, V_DIM), 0)
    sel_lane = lax.broadcasted_iota(jnp.int32, (BF16_SUBLANES, V_DIM), 1)
    half_sel = jnp.where(sel_row == sel_lane // HEAD_DIM, 1.0, 0.0).astype(jnp.bfloat16)

    def half_norms_sq(x):
        xf = x.astype(jnp.float32)
        return _scores(half_sel, (xf * xf).astype(jnp.bfloat16))

    @pl.when(qi == 0)
    def _():
        def key_tile(kt, best):
            k0 = pl.multiple_of(kt * tk, tk)
            return jnp.maximum(best, half_norms_sq(k_ref[0, 0, pl.ds(k0, tk), :]))
        best = lax.fori_loop(0, nk, key_tile, jnp.zeros((BF16_SUBLANES, tk), jnp.float32))
        kmax_sq = jnp.max(best, axis=1, keepdims=True)
        bias_max = jnp.max(jnp.max(strip_ref[0], axis=0, keepdims=True), axis=1, keepdims=True)
        info_row = lax.broadcasted_iota(jnp.int32, info_scr.shape, 0)
        info_scr[...] = jnp.where(info_row == 2, bias_max, jnp.broadcast_to(kmax_sq, info_scr.shape))

    q_sq = half_norms_sq(q)
    bias_max = info_scr[2:3, 0:1]
    ref = [jnp.sqrt(q_sq[mp:mp + 1, :] * info_scr[mp:mp + 1, 0:1]) * BOUND_SLACK + bias_max
           for mp in range(2)]

    def strip_row(kt):
        return pl.multiple_of(jnp.clip((kt * ratio - qi + ratio + 1) * tq, 0, tk + 3 * tq), tq)

    def bound_tile(kt):
        k0 = pl.multiple_of(kt * tk, tk)
        kblk = k_ref[0, 0, pl.ds(k0, tk), :]
        r0 = strip_row(kt)
        ss = [_scores(kblk, qmaps[mp]) for mp in range(2)]
        pv = [None, None]
        for c in range(tk // PV_CHUNK):
            keys = slice(c * PV_CHUNK, (c + 1) * PV_CHUNK)
            bias = strip_ref[0, pl.ds(r0 + c * PV_CHUNK, PV_CHUNK), :]
            for mp in range(2):
                p = jnp.exp2(ss[mp][keys] + bias - ref[mp]).astype(jnp.bfloat16)
                part = jnp.dot(vt_ref[0, 0, kt, :, keys], p, preferred_element_type=jnp.float32)
                pv[mp] = part if pv[mp] is None else pv[mp] + part
        for mp in range(2):
            acc_scr[mp] += pv[mp]

    def bound_tiles(g, carry):
        for t in range(group):
            bound_tile(g * group + t)
        return carry

    acc_scr[...] = jnp.zeros(acc_scr.shape, jnp.float32)
    lax.fori_loop(0, nk // group, bound_tiles, 0)

    def running_max_tile(kt, carry):
        k0 = pl.multiple_of(kt * tk, tk)
        kblk = k_ref[0, 0, pl.ds(k0, tk), :]
        bias = strip_ref[0, pl.ds(strip_row(kt), tk), :]
        for mp in range(2):
            s = _scores(kblk, qmaps[mp]) + bias
            s_scr[...] = s
            m_old = m_scr[mp]
            m_new = jnp.maximum(m_old, jnp.max(s, axis=0, keepdims=True))
            m_scr[mp] = m_new
            p = jnp.exp2(s_scr[...] - m_new).astype(jnp.bfloat16)
            acc_scr[mp] = jnp.exp2(m_old - m_new) * acc_scr[mp] + jnp.dot(
                vt_ref[0, 0, kt], p, preferred_element_type=jnp.float32)
        return carry

    denom = acc_scr[:, V_DIM:V_DIM + 1, :]
    trusted = (denom >= MIN_DENOMINATOR) & (denom <= MAX_DENOMINATOR)
    untrusted = jnp.sum(jnp.where(trusted, 0.0, 1.0))

    @pl.when(untrusted > 0.0)
    def _():
        m_scr[...] = jnp.full(m_scr.shape, NEG_BIG, jnp.float32)
        acc_scr[...] = jnp.zeros(acc_scr.shape, jnp.float32)
        lax.fori_loop(0, nk, running_max_tile, 0)

    lam = (jnp.exp(jnp.sum(lq1_ref[...] * lk1_ref[...], axis=-1, keepdims=True))
           - jnp.exp(jnp.sum(lq2_ref[...] * lk2_ref[...], axis=-1, keepdims=True)) + lambda_init)
    inv1 = 1.0 / acc_scr[0, V_DIM:V_DIM + 1, :]
    inv2 = lam / acc_scr[1, V_DIM:V_DIM + 1, :]
    o = acc_scr[0, :V_DIM, :] * inv1 - acc_scr[1, :V_DIM, :] * inv2
    o = o * lax.rsqrt(jnp.mean(o * o, axis=0, keepdims=True) + EPS) * subg_ref[...]
    o = o * (1.0 - lambda_init)
    o_ref[0] = (o.T * ga_ref[0].astype(jnp.float32)).astype(jnp.bfloat16)


def _attention(lam_vecs, q12, k12, vt, strips, ga, subln_g, lambda_init, tq, tk, group):
    batch, _, seq, _ = q12.shape
    nk = seq // tk
    assert nk % group == 0
    vec = pl.BlockSpec((1, HEAD_DIM), lambda b, h, i: (0, 0))
    return pl.pallas_call(
        functools.partial(_attn_kernel, tq=tq, tk=tk, group=group, lambda_init=lambda_init),
        grid=(batch, HEADS, seq // tq),
        in_specs=[
            vec, vec, vec, vec,
            pl.BlockSpec((1, 1, tq, V_DIM), lambda b, h, i: (b, h, i, 0)),
            pl.BlockSpec((1, 1, seq, V_DIM), lambda b, h, i: (b, h, 0, 0)),
            pl.BlockSpec((1, 1, nk, V_ROWS, tk), lambda b, h, i: (b, h, 0, 0, 0)),
            pl.BlockSpec((1, strips.shape[1], tq), lambda b, h, i: (h, 0, 0)),
            pl.BlockSpec((1, tq, V_DIM), lambda b, h, i: (b, i, h)),
            pl.BlockSpec((V_DIM, 1), lambda b, h, i: (0, 0)),
        ],
        out_specs=pl.BlockSpec((1, tq, V_DIM), lambda b, h, i: (b, i, h)),
        out_shape=jax.ShapeDtypeStruct((batch, seq, HEADS * V_DIM), jnp.bfloat16),
        scratch_shapes=[
            pltpu.VMEM((2, V_ROWS, tq), jnp.float32),
            pltpu.VMEM((BF16_SUBLANES, LANES), jnp.float32),
            pltpu.VMEM((tk, tq), jnp.float32),
            pltpu.VMEM((2, 1, tq), jnp.float32),
        ],
        compiler_params=pltpu.CompilerParams(
            dimension_semantics=("parallel", "parallel", "arbitrary"),
            vmem_limit_bytes=VMEM_LIMIT_BYTES),
        name="diff_attention",
    )(*lam_vecs, q12, k12, vt, strips, ga, subln_g)


def _out_kernel(x_ref, og_ref, sma_ref, ybg_ref, wpa_ref, wo_ref, gpost_ref, y_ref):
    y_a = jnp.dot(og_ref[0], wpa_ref[...], preferred_element_type=jnp.float32)
    merged = sma_ref[0] * y_a + ybg_ref[0]
    out = jnp.dot(merged.astype(jnp.bfloat16), wo_ref[...], preferred_element_type=jnp.float32)
    normed = out * lax.rsqrt(jnp.mean(out * out, axis=-1, keepdims=True) + EPS) * gpost_ref[...]
    y_ref[0] = x_ref[0] + normed


def _output(x, og, sma, ybg, w_pa_b, w_o_b, g_post, tm):
    batch, seq, _ = x.shape
    tok_spec = pl.BlockSpec((1, tm, D_MODEL), lambda b, i: (b, i, 0))
    return pl.pallas_call(
        _out_kernel,
        grid=(batch, seq // tm),
        in_specs=[tok_spec, tok_spec, tok_spec, tok_spec,
                  _resident((D_MODEL, D_MODEL)), _resident((D_MODEL, D_MODEL)),
                  _resident((1, D_MODEL))],
        out_specs=tok_spec,
        out_shape=jax.ShapeDtypeStruct(x.shape, x.dtype),
        compiler_params=pltpu.CompilerParams(
            dimension_semantics=("parallel", "parallel"), vmem_limit_bytes=VMEM_LIMIT_BYTES),
        name="merge_out",
    )(x, og, sma, ybg, w_pa_b, w_o_b, g_post)


def _head_major_columns():
    idx = np.arange(HEADS * V_DIM)
    hd, mp, d = idx // V_DIM, (idx % V_DIM) // HEAD_DIM, idx % HEAD_DIM
    return mp * (HEADS * HEAD_DIM) + hd * HEAD_DIM + d


def _encoder_layer(x, layer_idx, g_pre, w_in, lambda_q1, lambda_k1, lambda_q2, lambda_k2, subln_g,
                   w_pa, ln_g, ln_b, w_s, b_s, w_pb, w_o, g_post, rel_bias):
    seq = x.shape[1]
    tm, tq, tk, group = _tiles(seq)
    assert seq % tk == 0 and tk % tm == 0 and tk % tq == 0 and tq % LANES == 0 and tm % CHUNK == 0
    lambda_init = 0.8 - 0.6 * math.exp(-0.3 * layer_idx)
    bf16 = jnp.bfloat16
    row = lambda a: a.reshape(1, -1).astype(jnp.float32)

    perm = _head_major_columns()
    cols = np.concatenate([perm, D_MODEL + perm, np.arange(2 * D_MODEL, N_COL_BLOCKS * D_MODEL)])
    w_in_b = w_in[:, cols].astype(bf16)
    bs_b = jnp.broadcast_to(b_s.astype(jnp.float32)[:, :, None], (GROUPS, CHUNK, GROUP_DIM))

    q12, k12, vt, ga, sma, ybg = _project(
        x, row(g_pre), w_in_b, row(ln_g), row(ln_b), w_s.astype(bf16), bs_b, w_pb.astype(bf16), tm, tk)

    strips = _bias_strips(rel_bias, tq, tk)
    lam_vecs = [row(a) for a in (lambda_q1, lambda_k1, lambda_q2, lambda_k2)]
    subg_col = subln_g.reshape(V_DIM, 1).astype(jnp.float32)
    og = _attention(lam_vecs, q12, k12, vt, strips, ga, subg_col, lambda_init, tq, tk, group)

    return _output(x, og, sma, ybg, w_pa.astype(bf16), w_o.astype(bf16), row(g_post), tm)


def kernel(x_prompt, x_sample, g_pre, w_in, lambda_q1, lambda_k1, lambda_q2, lambda_k2, subln_g,
           w_pa, ln_g, ln_b, w_s, b_s, w_pb, w_o, g_post, rel_bias):
    y_prompt, y_sample = x_prompt, x_sample
    for l in range(g_pre.shape[0]):
        args = (g_pre[l], w_in[l], lambda_q1[l], lambda_k1[l], lambda_q2[l], lambda_k2[l], subln_g[l],
                w_pa[l], ln_g[l], ln_b[l], w_s[l], b_s[l], w_pb[l], w_o[l], g_post[l], rel_bias)
        y_prompt = _encoder_layer(y_prompt, l, *args)
        y_sample = _encoder_layer(y_sample, l, *args)
    return (y_prompt, y_sample)
```

```python
import functools
import math

import jax
import jax.numpy as jnp
import numpy as np
from jax import lax
from jax.experimental import pallas as pl
from jax.experimental.pallas import tpu as pltpu

D_MODEL = 1024
HEADS = 8
HEAD_DIM = 64
V_DIM = 2 * HEAD_DIM
GROUPS = 8
GROUP_DIM = 128
CHUNK = 128
REL_BUCKETS = 32
REL_MAX_DIST = 128
EPS = 1e-6
N_COL_BLOCKS = 9
LOG2E = math.log2(math.e)
LANES = 128
SUBLANES = 8
BF16_SUBLANES = 16
PV_CHUNK = 256
V_ROWS = V_DIM + BF16_SUBLANES
VMEM_LIMIT_BYTES = 56 * 1024 * 1024
NEG_BIG = -1e30
BOUND_SLACK = 1.0 + 2.0 ** -6
MIN_DENOMINATOR = 2.0 ** -64
MAX_DENOMINATOR = 2.0 ** 64


def _rel_bucket(rel):
    half = REL_BUCKETS // 2
    max_exact = half // 2
    n = jnp.abs(rel)
    nf = jnp.maximum(n, 1).astype(jnp.float32)
    large = max_exact + (jnp.log(nf / max_exact) / math.log(REL_MAX_DIST / max_exact)
                         * (half - max_exact)).astype(jnp.int32)
    large = jnp.minimum(large, half - 1)
    return jnp.where(rel > 0, half, 0) + jnp.where(n < max_exact, n, large)


def _tiles(seq):
    tm = min(512, seq)
    tq = min(256, seq)
    tk = min(1024, seq)
    nk = seq // tk
    group = min(16, nk)
    qpack = 4 if nk <= 4 else 2
    return tm, tq, tk, group, qpack


def _bias_strip_kernel(bucket_ref, table_ref, out_ref):
    h = pl.program_id(0)
    bucket = bucket_ref[...]
    acc = jnp.zeros(bucket.shape, jnp.float32)
    for b in range(REL_BUCKETS):
        acc = jnp.where(bucket == b, table_ref[b, h], acc)
    out_ref[0] = acc * LOG2E


def _bias_strips(rel_bias, tq, tk):
    n_rows = 2 * tk + 3 * tq
    rows_per_step = tq
    key = jnp.arange(n_rows, dtype=jnp.int32).reshape(n_rows, 1)
    qry = jnp.arange(tq, dtype=jnp.int32).reshape(1, tq)
    bucket = _rel_bucket(key - qry - tk - tq).astype(jnp.int32)
    return pl.pallas_call(
        _bias_strip_kernel,
        grid=(HEADS, n_rows // rows_per_step),
        in_specs=[
            pl.BlockSpec((rows_per_step, tq), lambda h, c: (c, 0)),
            pl.BlockSpec(memory_space=pltpu.SMEM),
        ],
        out_specs=pl.BlockSpec((1, rows_per_step, tq), lambda h, c: (h, c, 0)),
        out_shape=jax.ShapeDtypeStruct((HEADS, n_rows, tq), jnp.float32),
        name="bias_strips",
    )(bucket, rel_bias.astype(jnp.float32))


def _silu(x):
    return x * jax.nn.sigmoid(x)


def _proj_kernel(x_ref, gpre_ref, win_ref, lng_ref, lnb_ref, ws_ref, bs_ref, wpb_ref,
                 q_ref, k_ref, vt_ref, ga_ref, sma_ref, ybg_ref, t_scr):
    tm = x_ref.shape[1]
    x = x_ref[0]
    h = x * lax.rsqrt(jnp.mean(x * x, axis=-1, keepdims=True) + EPS) * gpre_ref[...]
    h = h.astype(jnp.bfloat16)

    def zcol(c):
        return jnp.dot(h, win_ref[:, c * D_MODEL:(c + 1) * D_MODEL],
                       preferred_element_type=jnp.float32)

    def store_heads(ref, z):
        zb = z.astype(jnp.bfloat16)
        for hd in range(HEADS):
            ref[0, hd] = zb[:, hd * V_DIM:(hd + 1) * V_DIM]

    store_heads(q_ref, zcol(0) * (LOG2E * HEAD_DIM ** -0.5))
    store_heads(k_ref, zcol(1))
    zv = zcol(2)
    ones = jnp.ones((BF16_SUBLANES, tm), jnp.bfloat16)
    for hd in range(HEADS):
        vt_ref[0, hd, 0, :V_DIM, :] = zv[:, hd * V_DIM:(hd + 1) * V_DIM].T.astype(jnp.bfloat16)
        vt_ref[0, hd, 0, V_DIM:, :] = ones
    ga_ref[0] = _silu(zcol(3)).astype(jnp.bfloat16)

    vb = zcol(5)
    mu = jnp.mean(vb, axis=-1, keepdims=True)
    vc = vb - mu
    vn = vc * lax.rsqrt(jnp.mean(vc * vc, axis=-1, keepdims=True) + EPS)
    vn = (vn * lng_ref[...] + lnb_ref[...]).astype(jnp.bfloat16)
    ug = zcol(4) * _silu(zcol(6))
    for c in range(tm // CHUNK):
        rows = slice(c * CHUNK, (c + 1) * CHUNK)
        for g in range(GROUPS):
            cols = slice(g * GROUP_DIM, (g + 1) * GROUP_DIM)
            s = jnp.dot(ws_ref[g], vn[rows, cols], preferred_element_type=jnp.float32) + bs_ref[g]
            t_scr[rows, cols] = (ug[rows, cols] * s).astype(jnp.bfloat16)
    y_b = jnp.dot(t_scr[...], wpb_ref[...], preferred_element_type=jnp.float32)

    sma_ref[0] = jax.nn.sigmoid(zcol(7))
    ybg_ref[0] = jax.nn.sigmoid(zcol(8)) * y_b


def _resident(shape):
    return pl.BlockSpec(shape, lambda *_: (0,) * len(shape), pipeline_mode=pl.Buffered(1))


def _project(x, g_pre, w_in_b, ln_g, ln_b, w_s_b, bs_b, w_pb_b, tm, tk):
    batch, seq, _ = x.shape
    per_k = tk // tm
    head_shape = jax.ShapeDtypeStruct((batch, HEADS, seq, V_DIM), jnp.bfloat16)
    head_spec = pl.BlockSpec((1, HEADS, tm, V_DIM), lambda b, i: (b, 0, i, 0))
    tok_spec = pl.BlockSpec((1, tm, D_MODEL), lambda b, i: (b, i, 0))
    return pl.pallas_call(
        _proj_kernel,
        grid=(batch, seq // tm),
        in_specs=[
            tok_spec,
            _resident((1, D_MODEL)),
            _resident((D_MODEL, N_COL_BLOCKS * D_MODEL)),
            _resident((1, D_MODEL)),
            _resident((1, D_MODEL)),
            _resident((GROUPS, CHUNK, CHUNK)),
            _resident((GROUPS, CHUNK, GROUP_DIM)),
            _resident((D_MODEL, D_MODEL)),
        ],
        out_specs=[
            head_spec, head_spec,
            pl.BlockSpec((1, HEADS, 1, V_ROWS, tm), lambda b, i: (b, 0, i // per_k, 0, i % per_k)),
            tok_spec, tok_spec, tok_spec],
        out_shape=[
            head_shape, head_shape,
            jax.ShapeDtypeStruct((batch, HEADS, seq // tk, V_ROWS, tk), jnp.bfloat16),
            jax.ShapeDtypeStruct((batch, seq, D_MODEL), jnp.bfloat16),
            jax.ShapeDtypeStruct((batch, seq, D_MODEL), jnp.float32),
            jax.ShapeDtypeStruct((batch, seq, D_MODEL), jnp.float32),
        ],
        scratch_shapes=[pltpu.VMEM((tm, D_MODEL), jnp.bfloat16)],
        compiler_params=pltpu.CompilerParams(
            dimension_semantics=("parallel", "parallel"), vmem_limit_bytes=VMEM_LIMIT_BYTES),
        name="proj_gmlp",
    )(x, g_pre, w_in_b, ln_g, ln_b, w_s_b, bs_b, w_pb_b)


def _scores(lhs, rhs):
    return lax.dot_general(lhs, rhs, (((1,), (1,)), ((), ())), preferred_element_type=jnp.float32)


def _attn_kernel(lq1_ref, lk1_ref, lq2_ref, lk2_ref, q_ref, k_ref, vt_ref, strip_ref,
                 ga_ref, subg_ref, o_ref, acc_scr, info_scr, s_scr, m_scr,
                 *, tq, tk, group, qpack, lambda_init):
    step = pl.program_id(2)
    nk = vt_ref.shape[2]
    ratio = tk // tq

    lane = lax.broadcasted_iota(jnp.int32, (tq, V_DIM), 1)

    def score_maps(j):
        q = q_ref[0, 0, pl.ds(pl.multiple_of(j * tq, tq), tq), :]
        zero = jnp.zeros_like(q)
        return q, (jnp.where(lane < HEAD_DIM, q, zero), jnp.where(lane >= HEAD_DIM, q, zero))

    sel_row = lax.broadcasted_iota(jnp.int32, (BF16_SUBLANES, V_DIM), 0)
    sel_lane = lax.broadcasted_iota(jnp.int32, (BF16_SUBLANES, V_DIM), 1)
    half_sel = jnp.where(sel_row == sel_lane // HEAD_DIM, 1.0, 0.0).astype(jnp.bfloat16)

    def half_norms_sq(x):
        xf = x.astype(jnp.float32)
        return _scores(half_sel, (xf * xf).astype(jnp.bfloat16))

    @pl.when(step == 0)
    def _():
        def key_tile(kt, best):
            k0 = pl.multiple_of(kt * tk, tk)
            return jnp.maximum(best, half_norms_sq(k_ref[0, 0, pl.ds(k0, tk), :]))
        best = lax.fori_loop(0, nk, key_tile, jnp.zeros((BF16_SUBLANES, tk), jnp.float32))
        kmax_sq = jnp.max(best, axis=1, keepdims=True)
        bias_max = jnp.max(jnp.max(strip_ref[0], axis=0, keepdims=True), axis=1, keepdims=True)
        info_row = lax.broadcasted_iota(jnp.int32, info_scr.shape, 0)
        info_scr[...] = jnp.where(info_row == 2, bias_max, jnp.broadcast_to(kmax_sq, info_scr.shape))

    def strip_row(j, kt):
        qt = step * qpack + j
        return pl.multiple_of(jnp.clip((kt * ratio - qt + ratio + 1) * tq, 0, tk + 3 * tq), tq)

    lam = (jnp.exp(jnp.sum(lq1_ref[...] * lk1_ref[...], axis=-1, keepdims=True))
           - jnp.exp(jnp.sum(lq2_ref[...] * lk2_ref[...], axis=-1, keepdims=True)) + lambda_init)

    def finalize(j):
        rows = pl.ds(pl.multiple_of(j * tq, tq), tq)
        inv1 = 1.0 / acc_scr[j, 0, V_DIM:V_DIM + 1, :]
        inv2 = lam / acc_scr[j, 1, V_DIM:V_DIM + 1, :]
        o = acc_scr[j, 0, :V_DIM, :] * inv1 - acc_scr[j, 1, :V_DIM, :] * inv2
        o = o * lax.rsqrt(jnp.mean(o * o, axis=0, keepdims=True) + EPS) * subg_ref[...]
        o = o * (1.0 - lambda_init)
        o_ref[0, rows, :] = (o.T * ga_ref[0, rows, :].astype(jnp.float32)).astype(jnp.bfloat16)

    def bound_pass(j):
        q, qmaps = score_maps(j)
        q_sq = half_norms_sq(q)
        bias_max = info_scr[2:3, 0:1]
        ref = [jnp.sqrt(q_sq[mp:mp + 1, :] * info_scr[mp:mp + 1, 0:1]) * BOUND_SLACK + bias_max
               for mp in range(2)]

        def bound_tile(kt, first):
            k0 = pl.multiple_of(kt * tk, tk)
            kblk = k_ref[0, 0, pl.ds(k0, tk), :]
            r0 = strip_row(j, kt)
            ss = [_scores(kblk, qmaps[mp]) for mp in range(2)]
            pv = [None, None]
            for c in range(tk // PV_CHUNK):
                keys = slice(c * PV_CHUNK, (c + 1) * PV_CHUNK)
                bias = strip_ref[0, pl.ds(r0 + c * PV_CHUNK, PV_CHUNK), :]
                for mp in range(2):
                    p = jnp.exp2(ss[mp][keys] + bias - ref[mp]).astype(jnp.bfloat16)
                    part = jnp.dot(vt_ref[0, 0, kt, :, keys], p, preferred_element_type=jnp.float32)
                    pv[mp] = part if pv[mp] is None else pv[mp] + part
            for mp in range(2):
                acc_scr[j, mp] = pv[mp] if first else acc_scr[j, mp] + pv[mp]

        if nk == group:
            for t in range(group):
                bound_tile(t, t == 0)
        else:
            acc_scr[j] = jnp.zeros(acc_scr.shape[1:], jnp.float32)

            def bound_tiles(g, carry):
                for t in range(group):
                    bound_tile(g * group + t, False)
                return carry
            lax.fori_loop(0, nk // group, bound_tiles, 0)

    def running_max_pass(j):
        _, qmaps = score_maps(j)
        m_scr[...] = jnp.full(m_scr.shape, NEG_BIG, jnp.float32)
        acc_scr[j] = jnp.zeros(acc_scr.shape[1:], jnp.float32)

        def tile(kt, carry):
            k0 = pl.multiple_of(kt * tk, tk)
            kblk = k_ref[0, 0, pl.ds(k0, tk), :]
            bias = strip_ref[0, pl.ds(strip_row(j, kt), tk), :]
            for mp in range(2):
                s = _scores(kblk, qmaps[mp]) + bias
                s_scr[...] = s
                m_old = m_scr[mp]
                m_new = jnp.maximum(m_old, jnp.max(s, axis=0, keepdims=True))
                m_scr[mp] = m_new
                p = jnp.exp2(s_scr[...] - m_new).astype(jnp.bfloat16)
                acc_scr[j, mp] = jnp.exp2(m_old - m_new) * acc_scr[j, mp] + jnp.dot(
                    vt_ref[0, 0, kt], p, preferred_element_type=jnp.float32)
            return carry
        lax.fori_loop(0, nk, tile, 0)

    untrusted = 0.0
    for j in range(qpack):
        bound_pass(j)
        denom = acc_scr[j, :, V_DIM:V_DIM + 1, :]
        trusted = (denom >= MIN_DENOMINATOR) & (denom <= MAX_DENOMINATOR)
        untrusted = untrusted + jnp.sum(jnp.where(trusted, 0.0, 1.0))
        finalize(j)

    @pl.when(untrusted > 0.0)
    def _():
        def redo(j, carry):
            running_max_pass(j)
            finalize(j)
            return carry
        lax.fori_loop(0, qpack, redo, 0)


def _attention(lam_vecs, q12, k12, vt, strips, ga, subln_g, lambda_init, tq, tk, group, qpack):
    batch, _, seq, _ = q12.shape
    nk = seq // tk
    tqs = tq * qpack
    assert nk % group == 0 and seq % tqs == 0
    vec = pl.BlockSpec((1, HEAD_DIM), lambda b, h, i: (0, 0))
    return pl.pallas_call(
        functools.partial(_attn_kernel, tq=tq, tk=tk, group=group, qpack=qpack, lambda_init=lambda_init),
        grid=(batch, HEADS, seq // tqs),
        in_specs=[
            vec, vec, vec, vec,
            pl.BlockSpec((1, 1, tqs, V_DIM), lambda b, h, i: (b, h, i, 0)),
            pl.BlockSpec((1, 1, seq, V_DIM), lambda b, h, i: (b, h, 0, 0)),
            pl.BlockSpec((1, 1, nk, V_ROWS, tk), lambda b, h, i: (b, h, 0, 0, 0)),
            pl.BlockSpec((1, strips.shape[1], tq), lambda b, h, i: (h, 0, 0)),
            pl.BlockSpec((1, tqs, V_DIM), lambda b, h, i: (b, i, h)),
            pl.BlockSpec((V_DIM, 1), lambda b, h, i: (0, 0)),
        ],
        out_specs=pl.BlockSpec((1, tqs, V_DIM), lambda b, h, i: (b, i, h)),
        out_shape=jax.ShapeDtypeStruct((batch, seq, HEADS * V_DIM), jnp.bfloat16),
        scratch_shapes=[
            pltpu.VMEM((qpack, 2, V_ROWS, tq), jnp.float32),
            pltpu.VMEM((BF16_SUBLANES, LANES), jnp.float32),
            pltpu.VMEM((tk, tq), jnp.float32),
            pltpu.VMEM((2, 1, tq), jnp.float32),
        ],
        compiler_params=pltpu.CompilerParams(
            dimension_semantics=("parallel", "parallel", "arbitrary"),
            vmem_limit_bytes=VMEM_LIMIT_BYTES),
        name="diff_attention",
    )(*lam_vecs, q12, k12, vt, strips, ga, subln_g)


def _out_kernel(x_ref, og_ref, sma_ref, ybg_ref, wpa_ref, wo_ref, gpost_ref, y_ref):
    y_a = jnp.dot(og_ref[0], wpa_ref[...], preferred_element_type=jnp.float32)
    merged = sma_ref[0] * y_a + ybg_ref[0]
    out = jnp.dot(merged.astype(jnp.bfloat16), wo_ref[...], preferred_element_type=jnp.float32)
    normed = out * lax.rsqrt(jnp.mean(out * out, axis=-1, keepdims=True) + EPS) * gpost_ref[...]
    y_ref[0] = x_ref[0] + normed


def _output(x, og, sma, ybg, w_pa_b, w_o_b, g_post, tm):
    batch, seq, _ = x.shape
    tok_spec = pl.BlockSpec((1, tm, D_MODEL), lambda b, i: (b, i, 0))
    return pl.pallas_call(
        _out_kernel,
        grid=(batch, seq // tm),
        in_specs=[tok_spec, tok_spec, tok_spec, tok_spec,
                  _resident((D_MODEL, D_MODEL)), _resident((D_MODEL, D_MODEL)),
                  _resident((1, D_MODEL))],
        out_specs=tok_spec,
        out_shape=jax.ShapeDtypeStruct(x.shape, x.dtype),
        compiler_params=pltpu.CompilerParams(
            dimension_semantics=("parallel", "parallel"), vmem_limit_bytes=VMEM_LIMIT_BYTES),
        name="merge_out",
    )(x, og, sma, ybg, w_pa_b, w_o_b, g_post)


def _head_major_columns():
    idx = np.arange(HEADS * V_DIM)
    hd, mp, d = idx // V_DIM, (idx % V_DIM) // HEAD_DIM, idx % HEAD_DIM
    return mp * (HEADS * HEAD_DIM) + hd * HEAD_DIM + d


def _encoder_layer(x, layer_idx, g_pre, w_in, lambda_q1, lambda_k1, lambda_q2, lambda_k2, subln_g,
                   w_pa, ln_g, ln_b, w_s, b_s, w_pb, w_o, g_post, rel_bias):
    seq = x.shape[1]
    tm, tq, tk, group, qpack = _tiles(seq)
    assert seq % tk == 0 and tk % tm == 0 and tk % tq == 0 and tq % LANES == 0 and tm % CHUNK == 0
    lambda_init = 0.8 - 0.6 * math.exp(-0.3 * layer_idx)
    bf16 = jnp.bfloat16
    row = lambda a: a.reshape(1, -1).astype(jnp.float32)

    perm = _head_major_columns()
    cols = np.concatenate([perm, D_MODEL + perm, np.arange(2 * D_MODEL, N_COL_BLOCKS * D_MODEL)])
    w_in_b = w_in[:, cols].astype(bf16)
    bs_b = jnp.broadcast_to(b_s.astype(jnp.float32)[:, :, None], (GROUPS, CHUNK, GROUP_DIM))

    q12, k12, vt, ga, sma, ybg = _project(
        x, row(g_pre), w_in_b, row(ln_g), row(ln_b), w_s.astype(bf16), bs_b, w_pb.astype(bf16), tm, tk)

    strips = _bias_strips(rel_bias, tq, tk)
    lam_vecs = [row(a) for a in (lambda_q1, lambda_k1, lambda_q2, lambda_k2)]
    subg_col = subln_g.reshape(V_DIM, 1).astype(jnp.float32)
    og = _attention(lam_vecs, q12, k12, vt, strips, ga, subg_col, lambda_init, tq, tk, group, qpack)

    return _output(x, og, sma, ybg, w_pa.astype(bf16), w_o.astype(bf16), row(g_post), tm)


def kernel(x_prompt, x_sample, g_pre, w_in, lambda_q1, lambda_k1, lambda_q2, lambda_k2, subln_g,
           w_pa, ln_g, ln_b, w_s, b_s, w_pb, w_o, g_post, rel_bias):
    y_prompt, y_sample = x_prompt, x_sample
    for l in range(g_pre.shape[0]):
        args = (g_pre[l], w_in[l], lambda_q1[l], lambda_k1[l], lambda_q2[l], lambda_k2[l], subln_g[l],
                w_pa[l], ln_g[l], ln_b[l], w_s[l], b_s[l], w_pb[l], w_o[l], g_post[l], rel_bias)
        y_prompt = _encoder_layer(y_prompt, l, *args)
        y_sample = _encoder_layer(y_sample, l, *args)
    return (y_prompt, y_sample)
```

```python
import functools
import math

import jax
import jax.numpy as jnp
import numpy as np
from jax import lax
from jax.experimental import pallas as pl
from jax.experimental.pallas import tpu as pltpu

D_MODEL = 1024
HEADS = 8
HEAD_DIM = 64
V_DIM = 2 * HEAD_DIM
GROUPS = 8
GROUP_DIM = 128
CHUNK = 128
REL_BUCKETS = 32
REL_MAX_DIST = 128
EPS = 1e-6
N_COL_BLOCKS = 9
LOG2E = math.log2(math.e)
LANES = 128
SUBLANES = 8
BF16_SUBLANES = 16
PV_CHUNK = 256
V_ROWS = V_DIM + BF16_SUBLANES
VMEM_LIMIT_BYTES = 56 * 1024 * 1024
NEG_BIG = -1e30
BOUND_SLACK = 1.0 + 2.0 ** -6
MIN_DENOMINATOR = 2.0 ** -64
MAX_DENOMINATOR = 2.0 ** 64


def _rel_bucket(rel):
    half = REL_BUCKETS // 2
    max_exact = half // 2
    n = jnp.abs(rel)
    nf = jnp.maximum(n, 1).astype(jnp.float32)
    large = max_exact + (jnp.log(nf / max_exact) / math.log(REL_MAX_DIST / max_exact)
                         * (half - max_exact)).astype(jnp.int32)
    large = jnp.minimum(large, half - 1)
    return jnp.where(rel > 0, half, 0) + jnp.where(n < max_exact, n, large)


def _tiles(seq):
    tm = min(512, seq)
    tq = min(256, seq)
    tk = min(1024, seq)
    nk = seq // tk
    group = min(16, nk)
    qpack = 8 if nk <= 4 else 4
    return tm, tq, tk, group, qpack


def _bias_strip_kernel(bucket_ref, table_ref, out_ref):
    h = pl.program_id(0)
    bucket = bucket_ref[...]
    acc = jnp.zeros(bucket.shape, jnp.float32)
    for b in range(REL_BUCKETS):
        acc = jnp.where(bucket == b, table_ref[b, h], acc)
    out_ref[0] = acc * LOG2E


def _bias_strips(rel_bias, tq, tk):
    n_rows = 2 * tk + 3 * tq
    rows_per_step = tq
    key = jnp.arange(n_rows, dtype=jnp.int32).reshape(n_rows, 1)
    qry = jnp.arange(tq, dtype=jnp.int32).reshape(1, tq)
    bucket = _rel_bucket(key - qry - tk - tq).astype(jnp.int32)
    return pl.pallas_call(
        _bias_strip_kernel,
        grid=(HEADS, n_rows // rows_per_step),
        in_specs=[
            pl.BlockSpec((rows_per_step, tq), lambda h, c: (c, 0)),
            pl.BlockSpec(memory_space=pltpu.SMEM),
        ],
        out_specs=pl.BlockSpec((1, rows_per_step, tq), lambda h, c: (h, c, 0)),
        out_shape=jax.ShapeDtypeStruct((HEADS, n_rows, tq), jnp.float32),
        name="bias_strips",
    )(bucket, rel_bias.astype(jnp.float32))


def _silu(x):
    return x * jax.nn.sigmoid(x)


def _proj_kernel(x_ref, gpre_ref, win_ref, lng_ref, lnb_ref, ws_ref, bs_ref, wpb_ref,
                 q_ref, k_ref, vt_ref, ga_ref, sma_ref, ybg_ref, t_scr):
    tm = x_ref.shape[1]
    x = x_ref[0]
    h = x * lax.rsqrt(jnp.mean(x * x, axis=-1, keepdims=True) + EPS) * gpre_ref[...]
    h = h.astype(jnp.bfloat16)

    def zcol(c):
        return jnp.dot(h, win_ref[:, c * D_MODEL:(c + 1) * D_MODEL],
                       preferred_element_type=jnp.float32)

    def store_heads(ref, z):
        zb = z.astype(jnp.bfloat16)
        for hd in range(HEADS):
            ref[0, hd] = zb[:, hd * V_DIM:(hd + 1) * V_DIM]

    store_heads(q_ref, zcol(0) * (LOG2E * HEAD_DIM ** -0.5))
    store_heads(k_ref, zcol(1))
    zv = zcol(2)
    ones = jnp.ones((BF16_SUBLANES, tm), jnp.bfloat16)
    for hd in range(HEADS):
        vt_ref[0, hd, 0, :V_DIM, :] = zv[:, hd * V_DIM:(hd + 1) * V_DIM].T.astype(jnp.bfloat16)
        vt_ref[0, hd, 0, V_DIM:, :] = ones
    ga_ref[0] = _silu(zcol(3)).astype(jnp.bfloat16)

    vb = zcol(5)
    mu = jnp.mean(vb, axis=-1, keepdims=True)
    vc = vb - mu
    vn = vc * lax.rsqrt(jnp.mean(vc * vc, axis=-1, keepdims=True) + EPS)
    vn = (vn * lng_ref[...] + lnb_ref[...]).astype(jnp.bfloat16)
    ug = zcol(4) * _silu(zcol(6))
    for c in range(tm // CHUNK):
        rows = slice(c * CHUNK, (c + 1) * CHUNK)
        for g in range(GROUPS):
            cols = slice(g * GROUP_DIM, (g + 1) * GROUP_DIM)
            s = jnp.dot(ws_ref[g], vn[rows, cols], preferred_element_type=jnp.float32) + bs_ref[g]
            t_scr[rows, cols] = (ug[rows, cols] * s).astype(jnp.bfloat16)
    y_b = jnp.dot(t_scr[...], wpb_ref[...], preferred_element_type=jnp.float32)

    sma_ref[0] = jax.nn.sigmoid(zcol(7))
    ybg_ref[0] = jax.nn.sigmoid(zcol(8)) * y_b


def _resident(shape):
    return pl.BlockSpec(shape, lambda *_: (0,) * len(shape), pipeline_mode=pl.Buffered(1))


def _project(x, g_pre, w_in_b, ln_g, ln_b, w_s_b, bs_b, w_pb_b, tm, tk):
    batch, seq, _ = x.shape
    per_k = tk // tm
    head_shape = jax.ShapeDtypeStruct((batch, HEADS, seq, V_DIM), jnp.bfloat16)
    head_spec = pl.BlockSpec((1, HEADS, tm, V_DIM), lambda b, i: (b, 0, i, 0))
    tok_spec = pl.BlockSpec((1, tm, D_MODEL), lambda b, i: (b, i, 0))
    return pl.pallas_call(
        _proj_kernel,
        grid=(batch, seq // tm),
        in_specs=[
            tok_spec,
            _resident((1, D_MODEL)),
            _resident((D_MODEL, N_COL_BLOCKS * D_MODEL)),
            _resident((1, D_MODEL)),
            _resident((1, D_MODEL)),
            _resident((GROUPS, CHUNK, CHUNK)),
            _resident((GROUPS, CHUNK, GROUP_DIM)),
            _resident((D_MODEL, D_MODEL)),
        ],
        out_specs=[
            head_spec, head_spec,
            pl.BlockSpec((1, HEADS, 1, V_ROWS, tm), lambda b, i: (b, 0, i // per_k, 0, i % per_k)),
            tok_spec, tok_spec, tok_spec],
        out_shape=[
            head_shape, head_shape,
            jax.ShapeDtypeStruct((batch, HEADS, seq // tk, V_ROWS, tk), jnp.bfloat16),
            jax.ShapeDtypeStruct((batch, seq, D_MODEL), jnp.bfloat16),
            jax.ShapeDtypeStruct((batch, seq, D_MODEL), jnp.float32),
            jax.ShapeDtypeStruct((batch, seq, D_MODEL), jnp.float32),
        ],
        scratch_shapes=[pltpu.VMEM((tm, D_MODEL), jnp.bfloat16)],
        compiler_params=pltpu.CompilerParams(
            dimension_semantics=("parallel", "parallel"), vmem_limit_bytes=VMEM_LIMIT_BYTES),
        name="proj_gmlp",
    )(x, g_pre, w_in_b, ln_g, ln_b, w_s_b, bs_b, w_pb_b)


def _scores(lhs, rhs):
    return lax.dot_general(lhs, rhs, (((1,), (1,)), ((), ())), preferred_element_type=jnp.float32)


def _attn_kernel(lq1_ref, lk1_ref, lq2_ref, lk2_ref, q_ref, k_ref, vt_ref, strip_ref,
                 ga_ref, subg_ref, o_ref, acc_scr, info_scr, s_scr, m_scr,
                 *, tq, tk, group, qpack, lambda_init):
    step = pl.program_id(2)
    nk = vt_ref.shape[2]
    ratio = tk // tq

    lane = lax.broadcasted_iota(jnp.int32, (tq, V_DIM), 1)

    def score_maps(j):
        q = q_ref[0, 0, pl.ds(pl.multiple_of(j * tq, tq), tq), :]
        zero = jnp.zeros_like(q)
        return q, (jnp.where(lane < HEAD_DIM, q, zero), jnp.where(lane >= HEAD_DIM, q, zero))

    sel_row = lax.broadcasted_iota(jnp.int32, (BF16_SUBLANES, V_DIM), 0)
    sel_lane = lax.broadcasted_iota(jnp.int32, (BF16_SUBLANES, V_DIM), 1)
    half_sel = jnp.where(sel_row == sel_lane // HEAD_DIM, 1.0, 0.0).astype(jnp.bfloat16)

    def half_norms_sq(x):
        xf = x.astype(jnp.float32)
        return _scores(half_sel, (xf * xf).astype(jnp.bfloat16))

    @pl.when(step == 0)
    def _():
        def key_tile(kt, best):
            k0 = pl.multiple_of(kt * tk, tk)
            return jnp.maximum(best, half_norms_sq(k_ref[0, 0, pl.ds(k0, tk), :]))
        best = lax.fori_loop(0, nk, key_tile, jnp.zeros((BF16_SUBLANES, tk), jnp.float32))
        kmax_sq = jnp.max(best, axis=1, keepdims=True)
        bias_max = jnp.max(jnp.max(strip_ref[0], axis=0, keepdims=True), axis=1, keepdims=True)
        info_row = lax.broadcasted_iota(jnp.int32, info_scr.shape, 0)
        info_scr[...] = jnp.where(info_row == 2, bias_max, jnp.broadcast_to(kmax_sq, info_scr.shape))

    def strip_row(j, kt):
        qt = step * qpack + j
        return pl.multiple_of(jnp.clip((kt * ratio - qt + ratio + 1) * tq, 0, tk + 3 * tq), tq)

    lam = (jnp.exp(jnp.sum(lq1_ref[...] * lk1_ref[...], axis=-1, keepdims=True))
           - jnp.exp(jnp.sum(lq2_ref[...] * lk2_ref[...], axis=-1, keepdims=True)) + lambda_init)

    def finalize(j):
        rows = pl.ds(pl.multiple_of(j * tq, tq), tq)
        inv1 = 1.0 / acc_scr[j, 0, V_DIM:V_DIM + 1, :]
        inv2 = lam / acc_scr[j, 1, V_DIM:V_DIM + 1, :]
        o = acc_scr[j, 0, :V_DIM, :] * inv1 - acc_scr[j, 1, :V_DIM, :] * inv2
        o = o * lax.rsqrt(jnp.mean(o * o, axis=0, keepdims=True) + EPS) * subg_ref[...]
        o = o * (1.0 - lambda_init)
        o_ref[0, rows, :] = (o.T * ga_ref[0, rows, :].astype(jnp.float32)).astype(jnp.bfloat16)

    def bound_pass(j):
        q, qmaps = score_maps(j)
        q_sq = half_norms_sq(q)
        bias_max = info_scr[2:3, 0:1]
        ref = [jnp.sqrt(q_sq[mp:mp + 1, :] * info_scr[mp:mp + 1, 0:1]) * BOUND_SLACK + bias_max
               for mp in range(2)]

        def bound_tile(kt, first):
            k0 = pl.multiple_of(kt * tk, tk)
            kblk = k_ref[0, 0, pl.ds(k0, tk), :]
            r0 = strip_row(j, kt)
            ss = [_scores(kblk, qmaps[mp]) for mp in range(2)]
            pv = [None, None]
            for c in range(tk // PV_CHUNK):
                keys = slice(c * PV_CHUNK, (c + 1) * PV_CHUNK)
                bias = strip_ref[0, pl.ds(r0 + c * PV_CHUNK, PV_CHUNK), :]
                for mp in range(2):
                    p = jnp.exp2(ss[mp][keys] + bias - ref[mp]).astype(jnp.bfloat16)
                    part = jnp.dot(vt_ref[0, 0, kt, :, keys], p, preferred_element_type=jnp.float32)
                    pv[mp] = part if pv[mp] is None else pv[mp] + part
            for mp in range(2):
                acc_scr[j, mp] = pv[mp] if first else acc_scr[j, mp] + pv[mp]

        if nk == group:
            for t in range(group):
                bound_tile(t, t == 0)
        else:
            acc_scr[j] = jnp.zeros(acc_scr.shape[1:], jnp.float32)

            def bound_tiles(g, carry):
                for t in range(group):
                    bound_tile(g * group + t, False)
                return carry
            lax.fori_loop(0, nk // group, bound_tiles, 0)

    def running_max_pass(j):
        _, qmaps = score_maps(j)
        m_scr[...] = jnp.full(m_scr.shape, NEG_BIG, jnp.float32)
        acc_scr[j] = jnp.zeros(acc_scr.shape[1:], jnp.float32)

        def tile(kt, carry):
            k0 = pl.multiple_of(kt * tk, tk)
            kblk = k_ref[0, 0, pl.ds(k0, tk), :]
            bias = strip_ref[0, pl.ds(strip_row(j, kt), tk), :]
            for mp in range(2):
                s = _scores(kblk, qmaps[mp]) + bias
                s_scr[...] = s
                m_old = m_scr[mp]
                m_new = jnp.maximum(m_old, jnp.max(s, axis=0, keepdims=True))
                m_scr[mp] = m_new
                p = jnp.exp2(s_scr[...] - m_new).astype(jnp.bfloat16)
                acc_scr[j, mp] = jnp.exp2(m_old - m_new) * acc_scr[j, mp] + jnp.dot(
                    vt_ref[0, 0, kt], p, preferred_element_type=jnp.float32)
            return carry
        lax.fori_loop(0, nk, tile, 0)

    untrusted = 0.0
    for j in range(qpack):
        bound_pass(j)
        denom = acc_scr[j, :, V_DIM:V_DIM + 1, :]
        trusted = (denom >= MIN_DENOMINATOR) & (denom <= MAX_DENOMINATOR)
        untrusted = untrusted + jnp.sum(jnp.where(trusted, 0.0, 1.0))
        finalize(j)

    @pl.when(untrusted > 0.0)
    def _():
        def redo(j, carry):
            running_max_pass(j)
            finalize(j)
            return carry
        lax.fori_loop(0, qpack, redo, 0)


def _attention(lam_vecs, q12, k12, vt, strips, ga, subln_g, lambda_init, tq, tk, group, qpack):
    batch, _, seq, _ = q12.shape
    nk = seq // tk
    tqs = tq * qpack
    assert nk % group == 0 and seq % tqs == 0
    vec = pl.BlockSpec((1, HEAD_DIM), lambda b, h, i: (0, 0))
    return pl.pallas_call(
        functools.partial(_attn_kernel, tq=tq, tk=tk, group=group, qpack=qpack, lambda_init=lambda_init),
        grid=(batch, HEADS, seq // tqs),
        in_specs=[
            vec, vec, vec, vec,
            pl.BlockSpec((1, 1, tqs, V_DIM), lambda b, h, i: (b, h, i, 0)),
            pl.BlockSpec((1, 1, seq, V_DIM), lambda b, h, i: (b, h, 0, 0)),
            pl.BlockSpec((1, 1, nk, V_ROWS, tk), lambda b, h, i: (b, h, 0, 0, 0)),
            pl.BlockSpec((1, strips.shape[1], tq), lambda b, h, i: (h, 0, 0)),
            pl.BlockSpec((1, tqs, V_DIM), lambda b, h, i: (b, i, h)),
            pl.BlockSpec((V_DIM, 1), lambda b, h, i: (0, 0)),
        ],
        out_specs=pl.BlockSpec((1, tqs, V_DIM), lambda b, h, i: (b, i, h)),
        out_shape=jax.ShapeDtypeStruct((batch, seq, HEADS * V_DIM), jnp.bfloat16),
        scratch_shapes=[
            pltpu.VMEM((qpack, 2, V_ROWS, tq), jnp.float32),
            pltpu.VMEM((BF16_SUBLANES, LANES), jnp.float32),
            pltpu.VMEM((tk, tq), jnp.float32),
            pltpu.VMEM((2, 1, tq), jnp.float32),
        ],
        compiler_params=pltpu.CompilerParams(
            dimension_semantics=("parallel", "parallel", "arbitrary"),
            vmem_limit_bytes=VMEM_LIMIT_BYTES),
        name="diff_attention",
    )(*lam_vecs, q12, k12, vt, strips, ga, subln_g)


def _out_kernel(x_ref, og_ref, sma_ref, ybg_ref, wpa_ref, wo_ref, gpost_ref, y_ref):
    y_a = jnp.dot(og_ref[0], wpa_ref[...], preferred_element_type=jnp.float32)
    merged = sma_ref[0] * y_a + ybg_ref[0]
    out = jnp.dot(merged.astype(jnp.bfloat16), wo_ref[...], preferred_element_type=jnp.float32)
    normed = out * lax.rsqrt(jnp.mean(out * out, axis=-1, keepdims=True) + EPS) * gpost_ref[...]
    y_ref[0] = x_ref[0] + normed


def _output(x, og, sma, ybg, w_pa_b, w_o_b, g_post, tm):
    batch, seq, _ = x.shape
    tok_spec = pl.BlockSpec((1, tm, D_MODEL), lambda b, i: (b, i, 0))
    return pl.pallas_call(
        _out_kernel,
        grid=(batch, seq // tm),
        in_specs=[tok_spec, tok_spec, tok_spec, tok_spec,
                  _resident((D_MODEL, D_MODEL)), _resident((D_MODEL, D_MODEL)),
                  _resident((1, D_MODEL))],
        out_specs=tok_spec,
        out_shape=jax.ShapeDtypeStruct(x.shape, x.dtype),
        compiler_params=pltpu.CompilerParams(
            dimension_semantics=("parallel", "parallel"), vmem_limit_bytes=VMEM_LIMIT_BYTES),
        name="merge_out",
    )(x, og, sma, ybg, w_pa_b, w_o_b, g_post)


def _head_major_columns():
    idx = np.arange(HEADS * V_DIM)
    hd, mp, d = idx // V_DIM, (idx % V_DIM) // HEAD_DIM, idx % HEAD_DIM
    return mp * (HEADS * HEAD_DIM) + hd * HEAD_DIM + d


def _encoder_layer(x, layer_idx, g_pre, w_in, lambda_q1, lambda_k1, lambda_q2, lambda_k2, subln_g,
                   w_pa, ln_g, ln_b, w_s, b_s, w_pb, w_o, g_post, rel_bias):
    seq = x.shape[1]
    tm, tq, tk, group, qpack = _tiles(seq)
    assert seq % tk == 0 and tk % tm == 0 and tk % tq == 0 and tq % LANES == 0 and tm % CHUNK == 0
    lambda_init = 0.8 - 0.6 * math.exp(-0.3 * layer_idx)
    bf16 = jnp.bfloat16
    row = lambda a: a.reshape(1, -1).astype(jnp.float32)

    perm = _head_major_columns()
    cols = np.concatenate([perm, D_MODEL + perm, np.arange(2 * D_MODEL, N_COL_BLOCKS * D_MODEL)])
    w_in_b = w_in[:, cols].astype(bf16)
    bs_b = jnp.broadcast_to(b_s.astype(jnp.float32)[:, :, None], (GROUPS, CHUNK, GROUP_DIM))

    q12, k12, vt, ga, sma, ybg = _project(
        x, row(g_pre), w_in_b, row(ln_g), row(ln_b), w_s.astype(bf16), bs_b, w_pb.astype(bf16), tm, tk)

    strips = _bias_strips(rel_bias, tq, tk)
    lam_vecs = [row(a) for a in (lambda_q1, lambda_k1, lambda_q2, lambda_k2)]
    subg_col = subln_g.reshape(V_DIM, 1).astype(jnp.float32)
    og = _attention(lam_vecs, q12, k12, vt, strips, ga, subg_col, lambda_init, tq, tk, group, qpack)

    return _output(x, og, sma, ybg, w_pa.astype(bf16), w_o.astype(bf16), row(g_post), tm)


def kernel(x_prompt, x_sample, g_pre, w_in, lambda_q1, lambda_k1, lambda_q2, lambda_k2, subln_g,
           w_pa, ln_g, ln_b, w_s, b_s, w_pb, w_o, g_post, rel_bias):
    y_prompt, y_sample = x_prompt, x_sample
    for l in range(g_pre.shape[0]):
        args = (g_pre[l], w_in[l], lambda_q1[l], lambda_k1[l], lambda_q2[l], lambda_k2[l], subln_g[l],
                w_pa[l], ln_g[l], ln_b[l], w_s[l], b_s[l], w_pb[l], w_o[l], g_post[l], rel_bias)
        y_prompt = _encoder_layer(y_prompt, l, *args)
        y_sample = _encoder_layer(y_sample, l, *args)
    return (y_prompt, y_sample)
```

```python
import functools
import math

import jax
import jax.numpy as jnp
from jax import lax
from jax.experimental import pallas as pl
from jax.experimental.pallas import tpu as pltpu

D_MODEL = 1024
HEADS = 8
HEAD_DIM = 64
V_DIM = 2 * HEAD_DIM
GROUPS = 8
GROUP_DIM = 128
CHUNK = 128
REL_BUCKETS = 32
REL_MAX_DIST = 128
EPS = 1e-6
N_COL_BLOCKS = 9
LOG2E = math.log2(math.e)
LANES = 128
SUBLANES = 8
BF16_SUBLANES = 16
PV_CHUNK = 256
V_ROWS = V_DIM + BF16_SUBLANES
VMEM_LIMIT_BYTES = 56 * 1024 * 1024
NEG_BIG = -1e30
BOUND_SLACK = 1.0 + 2.0 ** -6
MIN_DENOMINATOR = 2.0 ** -64
MAX_DENOMINATOR = 2.0 ** 64


def _rel_bucket(rel):
    half = REL_BUCKETS // 2
    max_exact = half // 2
    n = jnp.abs(rel)
    nf = jnp.maximum(n, 1).astype(jnp.float32)
    large = max_exact + (jnp.log(nf / max_exact) / math.log(REL_MAX_DIST / max_exact)
                         * (half - max_exact)).astype(jnp.int32)
    large = jnp.minimum(large, half - 1)
    return jnp.where(rel > 0, half, 0) + jnp.where(n < max_exact, n, large)


def _tiles(seq):
    tm = min(512, seq)
    tq = min(256, seq)
    tk = min(1024, seq)
    nk = seq // tk
    group = min(16, nk)
    qpack = 8 if nk <= 4 else 4
    return tm, tq, tk, group, qpack


def _bias_strip_kernel(bucket_ref, table_ref, out_ref):
    h = pl.program_id(0)
    bucket = bucket_ref[...]
    acc = jnp.zeros(bucket.shape, jnp.float32)
    for b in range(REL_BUCKETS):
        acc = jnp.where(bucket == b, table_ref[b, h], acc)
    out_ref[0] = acc * LOG2E


def _bias_strips(rel_bias, tq, tk):
    n_rows = 2 * tk + 3 * tq
    rows_per_step = tq
    key = jnp.arange(n_rows, dtype=jnp.int32).reshape(n_rows, 1)
    qry = jnp.arange(tq, dtype=jnp.int32).reshape(1, tq)
    bucket = _rel_bucket(key - qry - tk - tq).astype(jnp.int32)
    return pl.pallas_call(
        _bias_strip_kernel,
        grid=(HEADS, n_rows // rows_per_step),
        in_specs=[
            pl.BlockSpec((rows_per_step, tq), lambda h, c: (c, 0)),
            pl.BlockSpec(memory_space=pltpu.SMEM),
        ],
        out_specs=pl.BlockSpec((1, rows_per_step, tq), lambda h, c: (h, c, 0)),
        out_shape=jax.ShapeDtypeStruct((HEADS, n_rows, tq), jnp.float32),
        name="bias_strips",
    )(bucket, rel_bias.astype(jnp.float32))


def _silu(x):
    return x * jax.nn.sigmoid(x)


def _proj_kernel(x_ref, gpre_ref, win_ref, lng_ref, lnb_ref, ws_ref, bs_ref, wpb_ref,
                 q_ref, k_ref, vt_ref, ga_ref, sma_ref, ybg_ref, t_scr):
    tm = x_ref.shape[1]
    x = x_ref[0]
    h = x * lax.rsqrt(jnp.mean(x * x, axis=-1, keepdims=True) + EPS) * gpre_ref[...]
    h = h.astype(jnp.bfloat16)

    def zcol(c):
        return jnp.dot(h, win_ref[:, c * D_MODEL:(c + 1) * D_MODEL],
                       preferred_element_type=jnp.float32)

    def store_heads(ref, z):
        zb = z.astype(jnp.bfloat16)
        for hd in range(HEADS):
            ref[0, hd] = zb[:, hd * V_DIM:(hd + 1) * V_DIM]

    store_heads(q_ref, zcol(0) * (LOG2E * HEAD_DIM ** -0.5))
    store_heads(k_ref, zcol(1))
    zv = zcol(2)
    ones = jnp.ones((BF16_SUBLANES, tm), jnp.bfloat16)
    for hd in range(HEADS):
        vt_ref[0, hd, 0, :V_DIM, :] = zv[:, hd * V_DIM:(hd + 1) * V_DIM].T.astype(jnp.bfloat16)
        vt_ref[0, hd, 0, V_DIM:, :] = ones
    ga_ref[0] = _silu(zcol(3)).astype(jnp.bfloat16)

    vb = zcol(5)
    mu = jnp.mean(vb, axis=-1, keepdims=True)
    vc = vb - mu
    vn = vc * lax.rsqrt(jnp.mean(vc * vc, axis=-1, keepdims=True) + EPS)
    vn = (vn * lng_ref[...] + lnb_ref[...]).astype(jnp.bfloat16)
    ug = zcol(4) * _silu(zcol(6))
    for c in range(tm // CHUNK):
        rows = slice(c * CHUNK, (c + 1) * CHUNK)
        for g in range(GROUPS):
            cols = slice(g * GROUP_DIM, (g + 1) * GROUP_DIM)
            s = jnp.dot(ws_ref[g], vn[rows, cols], preferred_element_type=jnp.float32) + bs_ref[g]
            t_scr[rows, cols] = (ug[rows, cols] * s).astype(jnp.bfloat16)
    y_b = jnp.dot(t_scr[...], wpb_ref[...], preferred_element_type=jnp.float32)

    sma_ref[0] = jax.nn.sigmoid(zcol(7)).astype(jnp.bfloat16)
    ybg_ref[0] = (jax.nn.sigmoid(zcol(8)) * y_b).astype(jnp.bfloat16)


def _resident(shape):
    return pl.BlockSpec(shape, lambda *_: (0,) * len(shape), pipeline_mode=pl.Buffered(1))


def _project(x, g_pre, w_in_b, ln_g, ln_b, w_s_b, bs_b, w_pb_b, tm, tk):
    batch, seq, _ = x.shape
    per_k = tk // tm
    head_shape = jax.ShapeDtypeStruct((batch, HEADS, seq, V_DIM), jnp.bfloat16)
    head_spec = pl.BlockSpec((1, HEADS, tm, V_DIM), lambda b, i: (b, 0, i, 0))
    tok_spec = pl.BlockSpec((1, tm, D_MODEL), lambda b, i: (b, i, 0))
    return pl.pallas_call(
        _proj_kernel,
        grid=(batch, seq // tm),
        in_specs=[
            tok_spec,
            _resident((1, D_MODEL)),
            _resident((D_MODEL, N_COL_BLOCKS * D_MODEL)),
            _resident((1, D_MODEL)),
            _resident((1, D_MODEL)),
            _resident((GROUPS, CHUNK, CHUNK)),
            _resident((GROUPS, CHUNK, GROUP_DIM)),
            _resident((D_MODEL, D_MODEL)),
        ],
        out_specs=[
            head_spec, head_spec,
            pl.BlockSpec((1, HEADS, 1, V_ROWS, tm), lambda b, i: (b, 0, i // per_k, 0, i % per_k)),
            tok_spec, tok_spec, tok_spec],
        out_shape=[
            head_shape, head_shape,
            jax.ShapeDtypeStruct((batch, HEADS, seq // tk, V_ROWS, tk), jnp.bfloat16),
            jax.ShapeDtypeStruct((batch, seq, D_MODEL), jnp.bfloat16),
            jax.ShapeDtypeStruct((batch, seq, D_MODEL), jnp.bfloat16),
            jax.ShapeDtypeStruct((batch, seq, D_MODEL), jnp.bfloat16),
        ],
        scratch_shapes=[pltpu.VMEM((tm, D_MODEL), jnp.bfloat16)],
        compiler_params=pltpu.CompilerParams(
            dimension_semantics=("parallel", "parallel"), vmem_limit_bytes=VMEM_LIMIT_BYTES),
        name="proj_gmlp",
    )(x, g_pre, w_in_b, ln_g, ln_b, w_s_b, bs_b, w_pb_b)


def _scores(lhs, rhs):
    return lax.dot_general(lhs, rhs, (((1,), (1,)), ((), ())), preferred_element_type=jnp.float32)


def _attn_kernel(lq1_ref, lk1_ref, lq2_ref, lk2_ref, q_ref, k_ref, vt_ref, strip_ref,
                 ga_ref, subg_ref, o_ref, acc_scr, info_scr, s_scr, m_scr,
                 *, tq, tk, group, qpack, lambda_init):
    step = pl.program_id(2)
    nk = vt_ref.shape[2]
    ratio = tk // tq

    lane = lax.broadcasted_iota(jnp.int32, (tq, V_DIM), 1)

    def score_maps(j):
        q = q_ref[0, 0, pl.ds(pl.multiple_of(j * tq, tq), tq), :]
        zero = jnp.zeros_like(q)
        return q, (jnp.where(lane < HEAD_DIM, q, zero), jnp.where(lane >= HEAD_DIM, q, zero))

    sel_row = lax.broadcasted_iota(jnp.int32, (BF16_SUBLANES, V_DIM), 0)
    sel_lane = lax.broadcasted_iota(jnp.int32, (BF16_SUBLANES, V_DIM), 1)
    half_sel = jnp.where(sel_row == sel_lane // HEAD_DIM, 1.0, 0.0).astype(jnp.bfloat16)

    def half_norms_sq(x):
        xf = x.astype(jnp.float32)
        return _scores(half_sel, (xf * xf).astype(jnp.bfloat16))

    @pl.when(step == 0)
    def _():
        def key_tile(kt, best):
            k0 = pl.multiple_of(kt * tk, tk)
            return jnp.maximum(best, half_norms_sq(k_ref[0, 0, pl.ds(k0, tk), :]))
        best = lax.fori_loop(0, nk, key_tile, jnp.zeros((BF16_SUBLANES, tk), jnp.float32))
        kmax_sq = jnp.max(best, axis=1, keepdims=True)
        bias_max = jnp.max(jnp.max(strip_ref[0], axis=0, keepdims=True), axis=1, keepdims=True)
        info_row = lax.broadcasted_iota(jnp.int32, info_scr.shape, 0)
        info_scr[...] = jnp.where(info_row == 2, bias_max, jnp.broadcast_to(kmax_sq, info_scr.shape))

    def strip_row(j, kt):
        qt = step * qpack + j
        return pl.multiple_of(jnp.clip((kt * ratio - qt + ratio + 1) * tq, 0, tk + 3 * tq), tq)

    lam = (jnp.exp(jnp.sum(lq1_ref[...] * lk1_ref[...], axis=-1, keepdims=True))
           - jnp.exp(jnp.sum(lq2_ref[...] * lk2_ref[...], axis=-1, keepdims=True)) + lambda_init)

    def finalize(j):
        rows = pl.ds(pl.multiple_of(j * tq, tq), tq)
        inv1 = 1.0 / acc_scr[j, 0, V_DIM:V_DIM + 1, :]
        inv2 = lam / acc_scr[j, 1, V_DIM:V_DIM + 1, :]
        o = acc_scr[j, 0, :V_DIM, :] * inv1 - acc_scr[j, 1, :V_DIM, :] * inv2
        o = o * lax.rsqrt(jnp.mean(o * o, axis=0, keepdims=True) + EPS) * subg_ref[...]
        o = o * (1.0 - lambda_init)
        o_ref[0, rows, :] = (o.T * ga_ref[0, rows, :].astype(jnp.float32)).astype(jnp.bfloat16)

    def bound_pass(j):
        q, qmaps = score_maps(j)
        q_sq = half_norms_sq(q)
        bias_max = info_scr[2:3, 0:1]
        ref = [jnp.sqrt(q_sq[mp:mp + 1, :] * info_scr[mp:mp + 1, 0:1]) * BOUND_SLACK + bias_max
               for mp in range(2)]

        def bound_tile(kt, first):
            k0 = pl.multiple_of(kt * tk, tk)
            kblk = k_ref[0, 0, pl.ds(k0, tk), :]
            r0 = strip_row(j, kt)
            ss = [_scores(kblk, qmaps[mp]) for mp in range(2)]
            pv = [None, None]
            for c in range(tk // PV_CHUNK):
                keys = slice(c * PV_CHUNK, (c + 1) * PV_CHUNK)
                bias = strip_ref[0, pl.ds(r0 + c * PV_CHUNK, PV_CHUNK), :]
                for mp in range(2):
                    p = jnp.exp2(ss[mp][keys] + bias - ref[mp]).astype(jnp.bfloat16)
                    part = jnp.dot(vt_ref[0, 0, kt, :, keys], p, preferred_element_type=jnp.float32)
                    pv[mp] = part if pv[mp] is None else pv[mp] + part
            for mp in range(2):
                acc_scr[j, mp] = pv[mp] if first else acc_scr[j, mp] + pv[mp]

        if nk == group:
            for t in range(group):
                bound_tile(t, t == 0)
        else:
            acc_scr[j] = jnp.zeros(acc_scr.shape[1:], jnp.float32)

            def bound_tiles(g, carry):
                for t in range(group):
                    bound_tile(g * group + t, False)
                return carry
            lax.fori_loop(0, nk // group, bound_tiles, 0)

    def running_max_pass(j):
        _, qmaps = score_maps(j)
        m_scr[...] = jnp.full(m_scr.shape, NEG_BIG, jnp.float32)
        acc_scr[j] = jnp.zeros(acc_scr.shape[1:], jnp.float32)

        def tile(kt, carry):
            k0 = pl.multiple_of(kt * tk, tk)
            kblk = k_ref[0, 0, pl.ds(k0, tk), :]
            bias = strip_ref[0, pl.ds(strip_row(j, kt), tk), :]
            for mp in range(2):
                s = _scores(kblk, qmaps[mp]) + bias
                s_scr[...] = s
                m_old = m_scr[mp]
                m_new = jnp.maximum(m_old, jnp.max(s, axis=0, keepdims=True))
                m_scr[mp] = m_new
                p = jnp.exp2(s_scr[...] - m_new).astype(jnp.bfloat16)
                acc_scr[j, mp] = jnp.exp2(m_old - m_new) * acc_scr[j, mp] + jnp.dot(
                    vt_ref[0, 0, kt], p, preferred_element_type=jnp.float32)
            return carry
        lax.fori_loop(0, nk, tile, 0)

    untrusted = 0.0
    for j in range(qpack):
        bound_pass(j)
        denom = acc_scr[j, :, V_DIM:V_DIM + 1, :]
        trusted = (denom >= MIN_DENOMINATOR) & (denom <= MAX_DENOMINATOR)
        untrusted = untrusted + jnp.sum(jnp.where(trusted, 0.0, 1.0))
        finalize(j)

    @pl.when(untrusted > 0.0)
    def _():
        def redo(j, carry):
            running_max_pass(j)
            finalize(j)
            return carry
        lax.fori_loop(0, qpack, redo, 0)


def _attention(lam_vecs, q12, k12, vt, strips, ga, subln_g, lambda_init, tq, tk, group, qpack):
    batch, _, seq, _ = q12.shape
    nk = seq // tk
    tqs = tq * qpack
    assert nk % group == 0 and seq % tqs == 0
    vec = pl.BlockSpec((1, HEAD_DIM), lambda b, h, i: (0, 0))
    return pl.pallas_call(
        functools.partial(_attn_kernel, tq=tq, tk=tk, group=group, qpack=qpack, lambda_init=lambda_init),
        grid=(batch, HEADS, seq // tqs),
        in_specs=[
            vec, vec, vec, vec,
            pl.BlockSpec((1, 1, tqs, V_DIM), lambda b, h, i: (b, h, i, 0)),
            pl.BlockSpec((1, 1, seq, V_DIM), lambda b, h, i: (b, h, 0, 0)),
            pl.BlockSpec((1, 1, nk, V_ROWS, tk), lambda b, h, i: (b, h, 0, 0, 0)),
            pl.BlockSpec((1, strips.shape[1], tq), lambda b, h, i: (h, 0, 0)),
            pl.BlockSpec((1, tqs, V_DIM), lambda b, h, i: (b, i, h)),
            pl.BlockSpec((V_DIM, 1), lambda b, h, i: (0, 0)),
        ],
        out_specs=pl.BlockSpec((1, tqs, V_DIM), lambda b, h, i: (b, i, h)),
        out_shape=jax.ShapeDtypeStruct((batch, seq, HEADS * V_DIM), jnp.bfloat16),
        scratch_shapes=[
            pltpu.VMEM((qpack, 2, V_ROWS, tq), jnp.float32),
            pltpu.VMEM((BF16_SUBLANES, LANES), jnp.float32),
            pltpu.VMEM((tk, tq), jnp.float32),
            pltpu.VMEM((2, 1, tq), jnp.float32),
        ],
        compiler_params=pltpu.CompilerParams(
            dimension_semantics=("parallel", "parallel", "arbitrary"),
            vmem_limit_bytes=VMEM_LIMIT_BYTES),
        name="diff_attention",
    )(*lam_vecs, q12, k12, vt, strips, ga, subln_g)


def _out_kernel(x_ref, og_ref, sma_ref, ybg_ref, wpa_ref, wo_ref, gpost_ref, y_ref):
    y_a = jnp.dot(og_ref[0], wpa_ref[...], preferred_element_type=jnp.float32)
    merged = sma_ref[0].astype(jnp.float32) * y_a + ybg_ref[0].astype(jnp.float32)
    out = jnp.dot(merged.astype(jnp.bfloat16), wo_ref[...], preferred_element_type=jnp.float32)
    normed = out * lax.rsqrt(jnp.mean(out * out, axis=-1, keepdims=True) + EPS) * gpost_ref[...]
    y_ref[0] = x_ref[0] + normed


def _output(x, og, sma, ybg, w_pa_b, w_o_b, g_post, tm):
    batch, seq, _ = x.shape
    tok_spec = pl.BlockSpec((1, tm, D_MODEL), lambda b, i: (b, i, 0))
    return pl.pallas_call(
        _out_kernel,
        grid=(batch, seq // tm),
        in_specs=[tok_spec, tok_spec, tok_spec, tok_spec,
                  _resident((D_MODEL, D_MODEL)), _resident((D_MODEL, D_MODEL)),
                  _resident((1, D_MODEL))],
        out_specs=tok_spec,
        out_shape=jax.ShapeDtypeStruct(x.shape, x.dtype),
        compiler_params=pltpu.CompilerParams(
            dimension_semantics=("parallel", "parallel"), vmem_limit_bytes=VMEM_LIMIT_BYTES),
        name="merge_out",
    )(x, og, sma, ybg, w_pa_b, w_o_b, g_post)


def _head_major(w):
    return w.reshape(D_MODEL, 2, HEADS, HEAD_DIM).swapaxes(1, 2).reshape(D_MODEL, HEADS * V_DIM)


def _encoder_layer(x, layer_idx, g_pre, w_in, lambda_q1, lambda_k1, lambda_q2, lambda_k2, subln_g,
                   w_pa, ln_g, ln_b, w_s, b_s, w_pb, w_o, g_post, rel_bias):
    seq = x.shape[1]
    tm, tq, tk, group, qpack = _tiles(seq)
    assert seq % tk == 0 and tk % tm == 0 and tk % tq == 0 and tq % LANES == 0 and tm % CHUNK == 0
    lambda_init = 0.8 - 0.6 * math.exp(-0.3 * layer_idx)
    bf16 = jnp.bfloat16
    row = lambda a: a.reshape(1, -1).astype(jnp.float32)

    w_in_b = jnp.concatenate([_head_major(w_in[:, :D_MODEL]), _head_major(w_in[:, D_MODEL:2 * D_MODEL]),
                              w_in[:, 2 * D_MODEL:]], axis=1).astype(bf16)
    bs_b = jnp.broadcast_to(b_s.astype(jnp.float32)[:, :, None], (GROUPS, CHUNK, GROUP_DIM))

    q12, k12, vt, ga, sma, ybg = _project(
        x, row(g_pre), w_in_b, row(ln_g), row(ln_b), w_s.astype(bf16), bs_b, w_pb.astype(bf16), tm, tk)

    strips = _bias_strips(rel_bias, tq, tk)
    lam_vecs = [row(a) for a in (lambda_q1, lambda_k1, lambda_q2, lambda_k2)]
    subg_col = subln_g.reshape(V_DIM, 1).astype(jnp.float32)
    og = _attention(lam_vecs, q12, k12, vt, strips, ga, subg_col, lambda_init, tq, tk, group, qpack)

    return _output(x, og, sma, ybg, w_pa.astype(bf16), w_o.astype(bf16), row(g_post), tm)


def kernel(x_prompt, x_sample, g_pre, w_in, lambda_q1, lambda_k1, lambda_q2, lambda_k2, subln_g,
           w_pa, ln_g, ln_b, w_s, b_s, w_pb, w_o, g_post, rel_bias):
    y_prompt, y_sample = x_prompt, x_sample
    for l in range(g_pre.shape[0]):
        args = (g_pre[l], w_in[l], lambda_q1[l], lambda_k1[l], lambda_q2[l], lambda_k2[l], subln_g[l],
                w_pa[l], ln_g[l], ln_b[l], w_s[l], b_s[l], w_pb[l], w_o[l], g_post[l], rel_bias)
        y_prompt = _encoder_layer(y_prompt, l, *args)
        y_sample = _encoder_layer(y_sample, l, *args)
    return (y_prompt, y_sample)
```

```python
import functools
import math

import jax
import jax.numpy as jnp
from jax import lax
from jax.experimental import pallas as pl
from jax.experimental.pallas import tpu as pltpu

D_MODEL = 1024
HEADS = 8
HEAD_DIM = 64
V_DIM = 2 * HEAD_DIM
GROUPS = 8
GROUP_DIM = 128
CHUNK = 128
REL_BUCKETS = 32
REL_MAX_DIST = 128
EPS = 1e-6
N_COL_BLOCKS = 9
LOG2E = math.log2(math.e)
LANES = 128
SUBLANES = 8
BF16_SUBLANES = 16
PV_CHUNK = 256
V_ROWS = V_DIM + BF16_SUBLANES
VMEM_LIMIT_BYTES = 56 * 1024 * 1024
NEG_BIG = -1e30
BOUND_SLACK = 1.0 + 2.0 ** -6
MIN_DENOMINATOR = 2.0 ** -64
MAX_DENOMINATOR = 2.0 ** 64


def _rel_bucket(rel):
    half = REL_BUCKETS // 2
    max_exact = half // 2
    n = jnp.abs(rel)
    nf = jnp.maximum(n, 1).astype(jnp.float32)
    large = max_exact + (jnp.log(nf / max_exact) / math.log(REL_MAX_DIST / max_exact)
                         * (half - max_exact)).astype(jnp.int32)
    large = jnp.minimum(large, half - 1)
    return jnp.where(rel > 0, half, 0) + jnp.where(n < max_exact, n, large)


def _tiles(seq):
    tm = min(512, seq)
    tq = min(256, seq)
    tk = min(1024, seq)
    nk = seq // tk
    group = min(16, nk)
    qpack = 8 if nk <= 4 else 4
    return tm, tq, tk, group, qpack


def _bias_strip_kernel(bucket_ref, table_ref, out_ref):
    h = pl.program_id(0)
    bucket = bucket_ref[...]
    acc = jnp.zeros(bucket.shape, jnp.float32)
    for b in range(REL_BUCKETS):
        acc = jnp.where(bucket == b, table_ref[b, h], acc)
    out_ref[0] = acc * LOG2E


def _bias_strips(rel_bias, tq, tk):
    n_rows = 2 * tk + 3 * tq
    rows_per_step = tq
    key = jnp.arange(n_rows, dtype=jnp.int32).reshape(n_rows, 1)
    qry = jnp.arange(tq, dtype=jnp.int32).reshape(1, tq)
    bucket = _rel_bucket(key - qry - tk - tq).astype(jnp.int32)
    return pl.pallas_call(
        _bias_strip_kernel,
        grid=(HEADS, n_rows // rows_per_step),
        in_specs=[
            pl.BlockSpec((rows_per_step, tq), lambda h, c: (c, 0)),
            pl.BlockSpec(memory_space=pltpu.SMEM),
        ],
        out_specs=pl.BlockSpec((1, rows_per_step, tq), lambda h, c: (h, c, 0)),
        out_shape=jax.ShapeDtypeStruct((HEADS, n_rows, tq), jnp.float32),
        name="bias_strips",
    )(bucket, rel_bias.astype(jnp.float32))


def _silu(x):
    return x * jax.nn.sigmoid(x)


def _proj_kernel(x_ref, gpre_ref, win_ref, lng_ref, lnb_ref, ws_ref, bs_ref, wpb_ref,
                 q_ref, k_ref, vt_ref, ga_ref, sma_ref, ybg_ref, t_scr):
    tm = x_ref.shape[1]
    x = x_ref[0]
    h = x * lax.rsqrt(jnp.mean(x * x, axis=-1, keepdims=True) + EPS) * gpre_ref[...]
    h = h.astype(jnp.bfloat16)

    def zcol(c):
        return jnp.dot(h, win_ref[:, c * D_MODEL:(c + 1) * D_MODEL],
                       preferred_element_type=jnp.float32)

    def store_heads(ref, z):
        zb = z.astype(jnp.bfloat16)
        for hd in range(HEADS):
            ref[0, hd] = zb[:, hd * V_DIM:(hd + 1) * V_DIM]

    store_heads(q_ref, zcol(0) * (LOG2E * HEAD_DIM ** -0.5))
    store_heads(k_ref, zcol(1))
    zv = zcol(2)
    ones = jnp.ones((BF16_SUBLANES, tm), jnp.bfloat16)
    for hd in range(HEADS):
        vt_ref[0, hd, 0, :V_DIM, :] = zv[:, hd * V_DIM:(hd + 1) * V_DIM].T.astype(jnp.bfloat16)
        vt_ref[0, hd, 0, V_DIM:, :] = ones
    ga_ref[0] = _silu(zcol(3)).astype(jnp.bfloat16)

    vb = zcol(5)
    mu = jnp.mean(vb, axis=-1, keepdims=True)
    vc = vb - mu
    vn = vc * lax.rsqrt(jnp.mean(vc * vc, axis=-1, keepdims=True) + EPS)
    vn = (vn * lng_ref[...] + lnb_ref[...]).astype(jnp.bfloat16)
    ug = zcol(4) * _silu(zcol(6))
    for c in range(tm // CHUNK):
        rows = slice(c * CHUNK, (c + 1) * CHUNK)
        for g in range(GROUPS):
            cols = slice(g * GROUP_DIM, (g + 1) * GROUP_DIM)
            s = jnp.dot(ws_ref[g], vn[rows, cols], preferred_element_type=jnp.float32) + bs_ref[g]
            t_scr[rows, cols] = (ug[rows, cols] * s).astype(jnp.bfloat16)
    y_b = jnp.dot(t_scr[...], wpb_ref[...], preferred_element_type=jnp.float32)

    sma_ref[0] = jax.nn.sigmoid(zcol(7)).astype(jnp.bfloat16)
    ybg_ref[0] = (jax.nn.sigmoid(zcol(8)) * y_b).astype(jnp.bfloat16)


def _resident(shape):
    return pl.BlockSpec(shape, lambda *_: (0,) * len(shape), pipeline_mode=pl.Buffered(1))


def _project(x, g_pre, w_in_b, ln_g, ln_b, w_s_b, bs_b, w_pb_b, tm, tk):
    batch, seq, _ = x.shape
    per_k = tk // tm
    head_shape = jax.ShapeDtypeStruct((batch, HEADS, seq, V_DIM), jnp.bfloat16)
    head_spec = pl.BlockSpec((1, HEADS, tm, V_DIM), lambda b, i: (b, 0, i, 0))
    tok_spec = pl.BlockSpec((1, tm, D_MODEL), lambda b, i: (b, i, 0))
    return pl.pallas_call(
        _proj_kernel,
        grid=(batch, seq // tm),
        in_specs=[
            tok_spec,
            _resident((1, D_MODEL)),
            _resident((D_MODEL, N_COL_BLOCKS * D_MODEL)),
            _resident((1, D_MODEL)),
            _resident((1, D_MODEL)),
            _resident((GROUPS, CHUNK, CHUNK)),
            _resident((GROUPS, CHUNK, GROUP_DIM)),
            _resident((D_MODEL, D_MODEL)),
        ],
        out_specs=[
            head_spec, head_spec,
            pl.BlockSpec((1, HEADS, 1, V_ROWS, tm), lambda b, i: (b, 0, i // per_k, 0, i % per_k)),
            tok_spec, tok_spec, tok_spec],
        out_shape=[
            head_shape, head_shape,
            jax.ShapeDtypeStruct((batch, HEADS, seq // tk, V_ROWS, tk), jnp.bfloat16),
            jax.ShapeDtypeStruct((batch, seq, D_MODEL), jnp.bfloat16),
            jax.ShapeDtypeStruct((batch, seq, D_MODEL), jnp.bfloat16),
            jax.ShapeDtypeStruct((batch, seq, D_MODEL), jnp.bfloat16),
        ],
        scratch_shapes=[pltpu.VMEM((tm, D_MODEL), jnp.bfloat16)],
        compiler_params=pltpu.CompilerParams(
            dimension_semantics=("parallel", "parallel"), vmem_limit_bytes=VMEM_LIMIT_BYTES),
        name="proj_gmlp",
    )(x, g_pre, w_in_b, ln_g, ln_b, w_s_b, bs_b, w_pb_b)


def _scores(lhs, rhs):
    return lax.dot_general(lhs, rhs, (((1,), (1,)), ((), ())), preferred_element_type=jnp.float32)


def _attn_kernel(lq1_ref, lk1_ref, lq2_ref, lk2_ref, q_ref, k_ref, vt_ref, strip_ref,
                 ga_ref, subg_ref, o_ref, acc_scr, info_scr, s_scr, m_scr,
                 *, tq, tk, group, qpack, lambda_init):
    step = pl.program_id(2)
    nk = vt_ref.shape[2]
    ratio = tk // tq

    lane = lax.broadcasted_iota(jnp.int32, (tq, V_DIM), 1)

    def score_maps(j):
        q = q_ref[0, 0, pl.ds(pl.multiple_of(j * tq, tq), tq), :]
        zero = jnp.zeros_like(q)
        return q, (jnp.where(lane < HEAD_DIM, q, zero), jnp.where(lane >= HEAD_DIM, q, zero))

    sel_row = lax.broadcasted_iota(jnp.int32, (BF16_SUBLANES, V_DIM), 0)
    sel_lane = lax.broadcasted_iota(jnp.int32, (BF16_SUBLANES, V_DIM), 1)
    half_sel = jnp.where(sel_row == sel_lane // HEAD_DIM, 1.0, 0.0).astype(jnp.bfloat16)

    def half_norms_sq(x):
        xf = x.astype(jnp.float32)
        return _scores(half_sel, (xf * xf).astype(jnp.bfloat16))

    @pl.when(step == 0)
    def _():
        def key_tile(kt, best):
            k0 = pl.multiple_of(kt * tk, tk)
            return jnp.maximum(best, half_norms_sq(k_ref[0, 0, pl.ds(k0, tk), :]))
        best = lax.fori_loop(0, nk, key_tile, jnp.zeros((BF16_SUBLANES, tk), jnp.float32))
        kmax_sq = jnp.max(best, axis=1, keepdims=True)
        bias_max = jnp.max(jnp.max(strip_ref[0], axis=0, keepdims=True), axis=1, keepdims=True)
        info_row = lax.broadcasted_iota(jnp.int32, info_scr.shape, 0)
        info_scr[...] = jnp.where(info_row == 2, bias_max, jnp.broadcast_to(kmax_sq, info_scr.shape))

    def strip_row(j, kt):
        qt = step * qpack + j
        return pl.multiple_of(jnp.clip((kt * ratio - qt + ratio + 1) * tq, 0, tk + 3 * tq), tq)

    lam = (jnp.exp(jnp.sum(lq1_ref[...] * lk1_ref[...], axis=-1, keepdims=True))
           - jnp.exp(jnp.sum(lq2_ref[...] * lk2_ref[...], axis=-1, keepdims=True)) + lambda_init)

    def finalize(j):
        rows = pl.ds(pl.multiple_of(j * tq, tq), tq)
        inv1 = 1.0 / acc_scr[j, 0, V_DIM:V_DIM + 1, :]
        inv2 = lam / acc_scr[j, 1, V_DIM:V_DIM + 1, :]
        o = acc_scr[j, 0, :V_DIM, :] * inv1 - acc_scr[j, 1, :V_DIM, :] * inv2
        o = o * lax.rsqrt(jnp.mean(o * o, axis=0, keepdims=True) + EPS) * subg_ref[...]
        o = o * (1.0 - lambda_init)
        o_ref[0, rows, :] = (o.T * ga_ref[0, rows, :].astype(jnp.float32)).astype(jnp.bfloat16)

    def bound_pass(j):
        q, qmaps = score_maps(j)
        q_sq = half_norms_sq(q)
        bias_max = info_scr[2:3, 0:1]
        ref = [jnp.sqrt(q_sq[mp:mp + 1, :] * info_scr[mp:mp + 1, 0:1]) * BOUND_SLACK + bias_max
               for mp in range(2)]

        def bound_tile(kt, first, far_bias=None):
            k0 = pl.multiple_of(kt * tk, tk)
            kblk = k_ref[0, 0, pl.ds(k0, tk), :]
            ss = [_scores(kblk, qmaps[mp]) for mp in range(2)]
            if far_bias is None:
                r0 = strip_row(j, kt)
            else:
                shift = [far_bias - ref[mp] for mp in range(2)]
            pv = [None, None]
            for c in range(tk // PV_CHUNK):
                keys = slice(c * PV_CHUNK, (c + 1) * PV_CHUNK)
                if far_bias is None:
                    bias = strip_ref[0, pl.ds(r0 + c * PV_CHUNK, PV_CHUNK), :]
                for mp in range(2):
                    x = ss[mp][keys] + shift[mp] if far_bias is not None else ss[mp][keys] + bias - ref[mp]
                    p = jnp.exp2(x).astype(jnp.bfloat16)
                    part = jnp.dot(vt_ref[0, 0, kt, :, keys], p, preferred_element_type=jnp.float32)
                    pv[mp] = part if pv[mp] is None else pv[mp] + part
            for mp in range(2):
                acc_scr[j, mp] = pv[mp] if first else acc_scr[j, mp] + pv[mp]

        if nk == group:
            own = (step * qpack + j) // ratio
            n_rows = strip_ref.shape[1]
            far_left = strip_ref[0, 0:1, 0:1]
            far_right = strip_ref[0, n_rows - 1:n_rows, 0:1]
            for r in range(nk):
                wrapped = own + r >= nk
                kt = jnp.where(wrapped, own + r - nk, own + r)
                if 2 <= r <= nk - 2:
                    bound_tile(kt, r == 0, jnp.where(wrapped, far_left, far_right))
                else:
                    bound_tile(kt, r == 0)
        else:
            acc_scr[j] = jnp.zeros(acc_scr.shape[1:], jnp.float32)

            def bound_tiles(g, carry):
                for t in range(group):
                    bound_tile(g * group + t, False)
                return carry
            lax.fori_loop(0, nk // group, bound_tiles, 0)

    def running_max_pass(j):
        _, qmaps = score_maps(j)
        m_scr[...] = jnp.full(m_scr.shape, NEG_BIG, jnp.float32)
        acc_scr[j] = jnp.zeros(acc_scr.shape[1:], jnp.float32)

        def tile(kt, carry):
            k0 = pl.multiple_of(kt * tk, tk)
            kblk = k_ref[0, 0, pl.ds(k0, tk), :]
            bias = strip_ref[0, pl.ds(strip_row(j, kt), tk), :]
            for mp in range(2):
                s = _scores(kblk, qmaps[mp]) + bias
                s_scr[...] = s
                m_old = m_scr[mp]
                m_new = jnp.maximum(m_old, jnp.max(s, axis=0, keepdims=True))
                m_scr[mp] = m_new
                p = jnp.exp2(s_scr[...] - m_new).astype(jnp.bfloat16)
                acc_scr[j, mp] = jnp.exp2(m_old - m_new) * acc_scr[j, mp] + jnp.dot(
                    vt_ref[0, 0, kt], p, preferred_element_type=jnp.float32)
            return carry
        lax.fori_loop(0, nk, tile, 0)

    untrusted = 0.0
    for j in range(qpack):
        bound_pass(j)
        denom = acc_scr[j, :, V_DIM:V_DIM + 1, :]
        trusted = (denom >= MIN_DENOMINATOR) & (denom <= MAX_DENOMINATOR)
        untrusted = untrusted + jnp.sum(jnp.where(trusted, 0.0, 1.0))
        finalize(j)

    @pl.when(untrusted > 0.0)
    def _():
        def redo(j, carry):
            running_max_pass(j)
            finalize(j)
            return carry
        lax.fori_loop(0, qpack, redo, 0)


def _attention(lam_vecs, q12, k12, vt, strips, ga, subln_g, lambda_init, tq, tk, group, qpack):
    batch, _, seq, _ = q12.shape
    nk = seq // tk
    tqs = tq * qpack
    assert nk % group == 0 and seq % tqs == 0
    vec = pl.BlockSpec((1, HEAD_DIM), lambda b, h, i: (0, 0))
    return pl.pallas_call(
        functools.partial(_attn_kernel, tq=tq, tk=tk, group=group, qpack=qpack, lambda_init=lambda_init),
        grid=(batch, HEADS, seq // tqs),
        in_specs=[
            vec, vec, vec, vec,
            pl.BlockSpec((1, 1, tqs, V_DIM), lambda b, h, i: (b, h, i, 0)),
            pl.BlockSpec((1, 1, seq, V_DIM), lambda b, h, i: (b, h, 0, 0)),
            pl.BlockSpec((1, 1, nk, V_ROWS, tk), lambda b, h, i: (b, h, 0, 0, 0)),
            pl.BlockSpec((1, strips.shape[1], tq), lambda b, h, i: (h, 0, 0)),
            pl.BlockSpec((1, tqs, V_DIM), lambda b, h, i: (b, i, h)),
            pl.BlockSpec((V_DIM, 1), lambda b, h, i: (0, 0)),
        ],
        out_specs=pl.BlockSpec((1, tqs, V_DIM), lambda b, h, i: (b, i, h)),
        out_shape=jax.ShapeDtypeStruct((batch, seq, HEADS * V_DIM), jnp.bfloat16),
        scratch_shapes=[
            pltpu.VMEM((qpack, 2, V_ROWS, tq), jnp.float32),
            pltpu.VMEM((BF16_SUBLANES, LANES), jnp.float32),
            pltpu.VMEM((tk, tq), jnp.float32),
            pltpu.VMEM((2, 1, tq), jnp.float32),
        ],
        compiler_params=pltpu.CompilerParams(
            dimension_semantics=("parallel", "parallel", "arbitrary"),
            vmem_limit_bytes=VMEM_LIMIT_BYTES),
        name="diff_attention",
    )(*lam_vecs, q12, k12, vt, strips, ga, subln_g)


def _out_kernel(x_ref, og_ref, sma_ref, ybg_ref, wpa_ref, wo_ref, gpost_ref, y_ref):
    y_a = jnp.dot(og_ref[0], wpa_ref[...], preferred_element_type=jnp.float32)
    merged = sma_ref[0].astype(jnp.float32) * y_a + ybg_ref[0].astype(jnp.float32)
    out = jnp.dot(merged.astype(jnp.bfloat16), wo_ref[...], preferred_element_type=jnp.float32)
    normed = out * lax.rsqrt(jnp.mean(out * out, axis=-1, keepdims=True) + EPS) * gpost_ref[...]
    y_ref[0] = x_ref[0] + normed


def _output(x, og, sma, ybg, w_pa_b, w_o_b, g_post, tm):
    batch, seq, _ = x.shape
    tok_spec = pl.BlockSpec((1, tm, D_MODEL), lambda b, i: (b, i, 0))
    return pl.pallas_call(
        _out_kernel,
        grid=(batch, seq // tm),
        in_specs=[tok_spec, tok_spec, tok_spec, tok_spec,
                  _resident((D_MODEL, D_MODEL)), _resident((D_MODEL, D_MODEL)),
                  _resident((1, D_MODEL))],
        out_specs=tok_spec,
        out_shape=jax.ShapeDtypeStruct(x.shape, x.dtype),
        compiler_params=pltpu.CompilerParams(
            dimension_semantics=("parallel", "parallel"), vmem_limit_bytes=VMEM_LIMIT_BYTES),
        name="merge_out",
    )(x, og, sma, ybg, w_pa_b, w_o_b, g_post)


def _head_major(w):
    return w.reshape(D_MODEL, 2, HEADS, HEAD_DIM).swapaxes(1, 2).reshape(D_MODEL, HEADS * V_DIM)


def _encoder_layer(x, layer_idx, g_pre, w_in, lambda_q1, lambda_k1, lambda_q2, lambda_k2, subln_g,
                   w_pa, ln_g, ln_b, w_s, b_s, w_pb, w_o, g_post, rel_bias):
    seq = x.shape[1]
    tm, tq, tk, group, qpack = _tiles(seq)
    assert seq % tk == 0 and tk % tm == 0 and tk % tq == 0 and tq % LANES == 0 and tm % CHUNK == 0
    lambda_init = 0.8 - 0.6 * math.exp(-0.3 * layer_idx)
    bf16 = jnp.bfloat16
    row = lambda a: a.reshape(1, -1).astype(jnp.float32)

    w_in_b = jnp.concatenate([_head_major(w_in[:, :D_MODEL]), _head_major(w_in[:, D_MODEL:2 * D_MODEL]),
                              w_in[:, 2 * D_MODEL:]], axis=1).astype(bf16)
    bs_b = jnp.broadcast_to(b_s.astype(jnp.float32)[:, :, None], (GROUPS, CHUNK, GROUP_DIM))

    q12, k12, vt, ga, sma, ybg = _project(
        x, row(g_pre), w_in_b, row(ln_g), row(ln_b), w_s.astype(bf16), bs_b, w_pb.astype(bf16), tm, tk)

    strips = _bias_strips(rel_bias, tq, tk)
    lam_vecs = [row(a) for a in (lambda_q1, lambda_k1, lambda_q2, lambda_k2)]
    subg_col = subln_g.reshape(V_DIM, 1).astype(jnp.float32)
    og = _attention(lam_vecs, q12, k12, vt, strips, ga, subg_col, lambda_init, tq, tk, group, qpack)

    return _output(x, og, sma, ybg, w_pa.astype(bf16), w_o.astype(bf16), row(g_post), tm)


def kernel(x_prompt, x_sample, g_pre, w_in, lambda_q1, lambda_k1, lambda_q2, lambda_k2, subln_g,
           w_pa, ln_g, ln_b, w_s, b_s, w_pb, w_o, g_post, rel_bias):
    y_prompt, y_sample = x_prompt, x_sample
    for l in range(g_pre.shape[0]):
        args = (g_pre[l], w_in[l], lambda_q1[l], lambda_k1[l], lambda_q2[l], lambda_k2[l], subln_g[l],
                w_pa[l], ln_g[l], ln_b[l], w_s[l], b_s[l], w_pb[l], w_o[l], g_post[l], rel_bias)
        y_prompt = _encoder_layer(y_prompt, l, *args)
        y_sample = _encoder_layer(y_sample, l, *args)
    return (y_prompt, y_sample)
```

```python
import functools
import math

import jax
import jax.numpy as jnp
from jax import lax
from jax.experimental import pallas as pl
from jax.experimental.pallas import tpu as pltpu

D_MODEL = 1024
HEADS = 8
HEAD_DIM = 64
V_DIM = 2 * HEAD_DIM
GROUPS = 8
GROUP_DIM = 128
CHUNK = 128
REL_BUCKETS = 32
REL_MAX_DIST = 128
EPS = 1e-6
N_COL_BLOCKS = 9
LOG2E = math.log2(math.e)
LANES = 128
SUBLANES = 8
BF16_SUBLANES = 16
PV_CHUNK = 256
V_ROWS = V_DIM + BF16_SUBLANES
VMEM_LIMIT_BYTES = 56 * 1024 * 1024
NEG_BIG = -1e30
BOUND_SLACK = 1.0 + 2.0 ** -6
MIN_DENOMINATOR = 2.0 ** -64
MAX_DENOMINATOR = 2.0 ** 64


def _rel_bucket(rel):
    half = REL_BUCKETS // 2
    max_exact = half // 2
    n = jnp.abs(rel)
    nf = jnp.maximum(n, 1).astype(jnp.float32)
    large = max_exact + (jnp.log(nf / max_exact) / math.log(REL_MAX_DIST / max_exact)
                         * (half - max_exact)).astype(jnp.int32)
    large = jnp.minimum(large, half - 1)
    return jnp.where(rel > 0, half, 0) + jnp.where(n < max_exact, n, large)


def _tiles(seq):
    tm = min(512, seq)
    t_out = min(1024, seq)
    tq = min(256, seq)
    tk = min(1024, seq)
    nk = seq // tk
    group = min(16, nk)
    qpack = 8 if nk <= 4 else 4
    return tm, t_out, tq, tk, group, qpack


def _bias_strip_kernel(bucket_ref, table_ref, out_ref):
    h = pl.program_id(0)
    bucket = bucket_ref[...]
    acc = jnp.zeros(bucket.shape, jnp.float32)
    for b in range(REL_BUCKETS):
        acc = jnp.where(bucket == b, table_ref[b, h], acc)
    out_ref[0] = acc * LOG2E


def _bias_strips(rel_bias, tq, tk):
    n_rows = 2 * tk + 3 * tq
    rows_per_step = tq
    key = jnp.arange(n_rows, dtype=jnp.int32).reshape(n_rows, 1)
    qry = jnp.arange(tq, dtype=jnp.int32).reshape(1, tq)
    bucket = _rel_bucket(key - qry - tk - tq).astype(jnp.int32)
    return pl.pallas_call(
        _bias_strip_kernel,
        grid=(HEADS, n_rows // rows_per_step),
        in_specs=[
            pl.BlockSpec((rows_per_step, tq), lambda h, c: (c, 0)),
            pl.BlockSpec(memory_space=pltpu.SMEM),
        ],
        out_specs=pl.BlockSpec((1, rows_per_step, tq), lambda h, c: (h, c, 0)),
        out_shape=jax.ShapeDtypeStruct((HEADS, n_rows, tq), jnp.float32),
        name="bias_strips",
    )(bucket, rel_bias.astype(jnp.float32))


def _silu(x):
    return x * jax.nn.sigmoid(x)


def _proj_kernel(x_ref, gpre_ref, win_ref, lng_ref, lnb_ref, ws_ref, bs_ref, wpb_ref,
                 q_ref, k_ref, vt_ref, ga_ref, sma_ref, ybg_ref, t_scr):
    tm = x_ref.shape[1]
    x = x_ref[0]
    h = x * lax.rsqrt(jnp.mean(x * x, axis=-1, keepdims=True) + EPS) * gpre_ref[...]
    h = h.astype(jnp.bfloat16)

    def zcol(c):
        return jnp.dot(h, win_ref[:, c * D_MODEL:(c + 1) * D_MODEL],
                       preferred_element_type=jnp.float32)

    def store_heads(ref, z):
        zb = z.astype(jnp.bfloat16)
        for hd in range(HEADS):
            ref[0, hd] = zb[:, hd * V_DIM:(hd + 1) * V_DIM]

    store_heads(q_ref, zcol(0) * (LOG2E * HEAD_DIM ** -0.5))
    store_heads(k_ref, zcol(1))
    zv = zcol(2)
    ones = jnp.ones((BF16_SUBLANES, tm), jnp.bfloat16)
    for hd in range(HEADS):
        vt_ref[0, hd, 0, :V_DIM, :] = zv[:, hd * V_DIM:(hd + 1) * V_DIM].T.astype(jnp.bfloat16)
        vt_ref[0, hd, 0, V_DIM:, :] = ones
    ga_ref[0] = _silu(zcol(3)).astype(jnp.bfloat16)

    vb = zcol(5)
    mu = jnp.mean(vb, axis=-1, keepdims=True)
    vc = vb - mu
    vn = vc * lax.rsqrt(jnp.mean(vc * vc, axis=-1, keepdims=True) + EPS)
    vn = (vn * lng_ref[...] + lnb_ref[...]).astype(jnp.bfloat16)
    ug = zcol(4) * _silu(zcol(6))
    for c in range(tm // CHUNK):
        rows = slice(c * CHUNK, (c + 1) * CHUNK)
        for g in range(GROUPS):
            cols = slice(g * GROUP_DIM, (g + 1) * GROUP_DIM)
            s = jnp.dot(ws_ref[g], vn[rows, cols], preferred_element_type=jnp.float32) + bs_ref[g]
            t_scr[rows, cols] = (ug[rows, cols] * s).astype(jnp.bfloat16)
    y_b = jnp.dot(t_scr[...], wpb_ref[...], preferred_element_type=jnp.float32)

    sma_ref[0] = jax.nn.sigmoid(zcol(7)).astype(jnp.bfloat16)
    ybg_ref[0] = (jax.nn.sigmoid(zcol(8)) * y_b).astype(jnp.bfloat16)


def _resident(shape):
    return pl.BlockSpec(shape, lambda *_: (0,) * len(shape), pipeline_mode=pl.Buffered(1))


def _project(x, g_pre, w_in_b, ln_g, ln_b, w_s_b, bs_b, w_pb_b, tm, tk):
    batch, seq, _ = x.shape
    per_k = tk // tm
    head_shape = jax.ShapeDtypeStruct((batch, HEADS, seq, V_DIM), jnp.bfloat16)
    head_spec = pl.BlockSpec((1, HEADS, tm, V_DIM), lambda b, i: (b, 0, i, 0))
    tok_spec = pl.BlockSpec((1, tm, D_MODEL), lambda b, i: (b, i, 0))
    return pl.pallas_call(
        _proj_kernel,
        grid=(batch, seq // tm),
        in_specs=[
            tok_spec,
            _resident((1, D_MODEL)),
            _resident((D_MODEL, N_COL_BLOCKS * D_MODEL)),
            _resident((1, D_MODEL)),
            _resident((1, D_MODEL)),
            _resident((GROUPS, CHUNK, CHUNK)),
            _resident((GROUPS, CHUNK, GROUP_DIM)),
            _resident((D_MODEL, D_MODEL)),
        ],
        out_specs=[
            head_spec, head_spec,
            pl.BlockSpec((1, HEADS, 1, V_ROWS, tm), lambda b, i: (b, 0, i // per_k, 0, i % per_k)),
            tok_spec, tok_spec, tok_spec],
        out_shape=[
            head_shape, head_shape,
            jax.ShapeDtypeStruct((batch, HEADS, seq // tk, V_ROWS, tk), jnp.bfloat16),
            jax.ShapeDtypeStruct((batch, seq, D_MODEL), jnp.bfloat16),
            jax.ShapeDtypeStruct((batch, seq, D_MODEL), jnp.bfloat16),
            jax.ShapeDtypeStruct((batch, seq, D_MODEL), jnp.bfloat16),
        ],
        scratch_shapes=[pltpu.VMEM((tm, D_MODEL), jnp.bfloat16)],
        compiler_params=pltpu.CompilerParams(
            dimension_semantics=("parallel", "parallel"), vmem_limit_bytes=VMEM_LIMIT_BYTES),
        name="proj_gmlp",
    )(x, g_pre, w_in_b, ln_g, ln_b, w_s_b, bs_b, w_pb_b)


def _scores(lhs, rhs):
    return lax.dot_general(lhs, rhs, (((1,), (1,)), ((), ())), preferred_element_type=jnp.float32)


def _attn_kernel(lq1_ref, lk1_ref, lq2_ref, lk2_ref, q_ref, k_ref, vt_ref, strip_ref,
                 ga_ref, subg_ref, o_ref, acc_scr, info_scr, s_scr, m_scr,
                 *, tq, tk, group, qpack, lambda_init):
    step = pl.program_id(2)
    nk = vt_ref.shape[2]
    ratio = tk // tq

    lane = lax.broadcasted_iota(jnp.int32, (tq, V_DIM), 1)

    def score_maps(j):
        q = q_ref[0, 0, pl.ds(pl.multiple_of(j * tq, tq), tq), :]
        zero = jnp.zeros_like(q)
        return q, (jnp.where(lane < HEAD_DIM, q, zero), jnp.where(lane >= HEAD_DIM, q, zero))

    sel_row = lax.broadcasted_iota(jnp.int32, (BF16_SUBLANES, V_DIM), 0)
    sel_lane = lax.broadcasted_iota(jnp.int32, (BF16_SUBLANES, V_DIM), 1)
    half_sel = jnp.where(sel_row == sel_lane // HEAD_DIM, 1.0, 0.0).astype(jnp.bfloat16)

    def half_norms_sq(x):
        xf = x.astype(jnp.float32)
        return _scores(half_sel, (xf * xf).astype(jnp.bfloat16))

    @pl.when(step == 0)
    def _():
        def key_tile(kt, best):
            k0 = pl.multiple_of(kt * tk, tk)
            return jnp.maximum(best, half_norms_sq(k_ref[0, 0, pl.ds(k0, tk), :]))
        best = lax.fori_loop(0, nk, key_tile, jnp.zeros((BF16_SUBLANES, tk), jnp.float32))
        kmax_sq = jnp.max(best, axis=1, keepdims=True)
        bias_max = jnp.max(jnp.max(strip_ref[0], axis=0, keepdims=True), axis=1, keepdims=True)
        info_row = lax.broadcasted_iota(jnp.int32, info_scr.shape, 0)
        info_scr[...] = jnp.where(info_row == 2, bias_max, jnp.broadcast_to(kmax_sq, info_scr.shape))

    def strip_row(j, kt):
        qt = step * qpack + j
        return pl.multiple_of(jnp.clip((kt * ratio - qt + ratio + 1) * tq, 0, tk + 3 * tq), tq)

    lam = (jnp.exp(jnp.sum(lq1_ref[...] * lk1_ref[...], axis=-1, keepdims=True))
           - jnp.exp(jnp.sum(lq2_ref[...] * lk2_ref[...], axis=-1, keepdims=True)) + lambda_init)

    def finalize(j):
        rows = pl.ds(pl.multiple_of(j * tq, tq), tq)
        inv1 = 1.0 / acc_scr[j, 0, V_DIM:V_DIM + 1, :]
        inv2 = lam / acc_scr[j, 1, V_DIM:V_DIM + 1, :]
        o = acc_scr[j, 0, :V_DIM, :] * inv1 - acc_scr[j, 1, :V_DIM, :] * inv2
        o = o * lax.rsqrt(jnp.mean(o * o, axis=0, keepdims=True) + EPS) * subg_ref[...]
        o = o * (1.0 - lambda_init)
        o_ref[0, rows, :] = (o.T * ga_ref[0, rows, :].astype(jnp.float32)).astype(jnp.bfloat16)

    def bound_pass(j):
        q, qmaps = score_maps(j)
        q_sq = half_norms_sq(q)
        bias_max = info_scr[2:3, 0:1]
        ref = [jnp.sqrt(q_sq[mp:mp + 1, :] * info_scr[mp:mp + 1, 0:1]) * BOUND_SLACK + bias_max
               for mp in range(2)]

        def bound_tile(kt, first, far_bias=None):
            k0 = pl.multiple_of(kt * tk, tk)
            kblk = k_ref[0, 0, pl.ds(k0, tk), :]
            ss = [_scores(kblk, qmaps[mp]) for mp in range(2)]
            if far_bias is None:
                r0 = strip_row(j, kt)
            else:
                shift = [far_bias - ref[mp] for mp in range(2)]
            pv = [None, None]
            for c in range(tk // PV_CHUNK):
                keys = slice(c * PV_CHUNK, (c + 1) * PV_CHUNK)
                if far_bias is None:
                    bias = strip_ref[0, pl.ds(r0 + c * PV_CHUNK, PV_CHUNK), :]
                for mp in range(2):
                    x = ss[mp][keys] + shift[mp] if far_bias is not None else ss[mp][keys] + bias - ref[mp]
                    p = jnp.exp2(x).astype(jnp.bfloat16)
                    part = jnp.dot(vt_ref[0, 0, kt, :, keys], p, preferred_element_type=jnp.float32)
                    pv[mp] = part if pv[mp] is None else pv[mp] + part
            for mp in range(2):
                acc_scr[j, mp] = pv[mp] if first else acc_scr[j, mp] + pv[mp]

        if nk == group:
            own = (step * qpack + j) // ratio
            n_rows = strip_ref.shape[1]
            far_left = strip_ref[0, 0:1, 0:1]
            far_right = strip_ref[0, n_rows - 1:n_rows, 0:1]
            for r in range(nk):
                wrapped = own + r >= nk
                kt = jnp.where(wrapped, own + r - nk, own + r)
                if 2 <= r <= nk - 2:
                    bound_tile(kt, r == 0, jnp.where(wrapped, far_left, far_right))
                else:
                    bound_tile(kt, r == 0)
        else:
            acc_scr[j] = jnp.zeros(acc_scr.shape[1:], jnp.float32)

            def bound_tiles(g, carry):
                for t in range(group):
                    bound_tile(g * group + t, False)
                return carry
            lax.fori_loop(0, nk // group, bound_tiles, 0)

    def running_max_pass(j):
        _, qmaps = score_maps(j)
        m_scr[...] = jnp.full(m_scr.shape, NEG_BIG, jnp.float32)
        acc_scr[j] = jnp.zeros(acc_scr.shape[1:], jnp.float32)

        def tile(kt, carry):
            k0 = pl.multiple_of(kt * tk, tk)
            kblk = k_ref[0, 0, pl.ds(k0, tk), :]
            bias = strip_ref[0, pl.ds(strip_row(j, kt), tk), :]
            for mp in range(2):
                s = _scores(kblk, qmaps[mp]) + bias
                s_scr[...] = s
                m_old = m_scr[mp]
                m_new = jnp.maximum(m_old, jnp.max(s, axis=0, keepdims=True))
                m_scr[mp] = m_new
                p = jnp.exp2(s_scr[...] - m_new).astype(jnp.bfloat16)
                acc_scr[j, mp] = jnp.exp2(m_old - m_new) * acc_scr[j, mp] + jnp.dot(
                    vt_ref[0, 0, kt], p, preferred_element_type=jnp.float32)
            return carry
        lax.fori_loop(0, nk, tile, 0)

    untrusted = 0.0
    for j in range(qpack):
        bound_pass(j)
        denom = acc_scr[j, :, V_DIM:V_DIM + 1, :]
        trusted = (denom >= MIN_DENOMINATOR) & (denom <= MAX_DENOMINATOR)
        untrusted = untrusted + jnp.sum(jnp.where(trusted, 0.0, 1.0))
        finalize(j)

    @pl.when(untrusted > 0.0)
    def _():
        def redo(j, carry):
            running_max_pass(j)
            finalize(j)
            return carry
        lax.fori_loop(0, qpack, redo, 0)


def _attention(lam_vecs, q12, k12, vt, strips, ga, subln_g, lambda_init, tq, tk, group, qpack):
    batch, _, seq, _ = q12.shape
    nk = seq // tk
    tqs = tq * qpack
    assert nk % group == 0 and seq % tqs == 0
    vec = pl.BlockSpec((1, HEAD_DIM), lambda b, h, i: (0, 0))
    return pl.pallas_call(
        functools.partial(_attn_kernel, tq=tq, tk=tk, group=group, qpack=qpack, lambda_init=lambda_init),
        grid=(batch, HEADS, seq // tqs),
        in_specs=[
            vec, vec, vec, vec,
            pl.BlockSpec((1, 1, tqs, V_DIM), lambda b, h, i: (b, h, i, 0)),
            pl.BlockSpec((1, 1, seq, V_DIM), lambda b, h, i: (b, h, 0, 0)),
            pl.BlockSpec((1, 1, nk, V_ROWS, tk), lambda b, h, i: (b, h, 0, 0, 0)),
            pl.BlockSpec((1, strips.shape[1], tq), lambda b, h, i: (h, 0, 0)),
            pl.BlockSpec((1, tqs, V_DIM), lambda b, h, i: (b, i, h)),
            pl.BlockSpec((V_DIM, 1), lambda b, h, i: (0, 0)),
        ],
        out_specs=pl.BlockSpec((1, tqs, V_DIM), lambda b, h, i: (b, i, h)),
        out_shape=jax.ShapeDtypeStruct((batch, seq, HEADS * V_DIM), jnp.bfloat16),
        scratch_shapes=[
            pltpu.VMEM((qpack, 2, V_ROWS, tq), jnp.float32),
            pltpu.VMEM((BF16_SUBLANES, LANES), jnp.float32),
            pltpu.VMEM((tk, tq), jnp.float32),
            pltpu.VMEM((2, 1, tq), jnp.float32),
        ],
        compiler_params=pltpu.CompilerParams(
            dimension_semantics=("parallel", "parallel", "arbitrary"),
            vmem_limit_bytes=VMEM_LIMIT_BYTES),
        name="diff_attention",
    )(*lam_vecs, q12, k12, vt, strips, ga, subln_g)


def _out_kernel(x_ref, og_ref, sma_ref, ybg_ref, wpa_ref, wo_ref, gpost_ref, y_ref):
    y_a = jnp.dot(og_ref[0], wpa_ref[...], preferred_element_type=jnp.float32)
    merged = sma_ref[0].astype(jnp.float32) * y_a + ybg_ref[0].astype(jnp.float32)
    out = jnp.dot(merged.astype(jnp.bfloat16), wo_ref[...], preferred_element_type=jnp.float32)
    normed = out * lax.rsqrt(jnp.mean(out * out, axis=-1, keepdims=True) + EPS) * gpost_ref[...]
    y_ref[0] = x_ref[0] + normed


def _output(x, og, sma, ybg, w_pa_b, w_o_b, g_post, tm):
    batch, seq, _ = x.shape
    tok_spec = pl.BlockSpec((1, tm, D_MODEL), lambda b, i: (b, i, 0))
    return pl.pallas_call(
        _out_kernel,
        grid=(batch, seq // tm),
        in_specs=[tok_spec, tok_spec, tok_spec, tok_spec,
                  _resident((D_MODEL, D_MODEL)), _resident((D_MODEL, D_MODEL)),
                  _resident((1, D_MODEL))],
        out_specs=tok_spec,
        out_shape=jax.ShapeDtypeStruct(x.shape, x.dtype),
        compiler_params=pltpu.CompilerParams(
            dimension_semantics=("parallel", "parallel"), vmem_limit_bytes=VMEM_LIMIT_BYTES),
        name="merge_out",
    )(x, og, sma, ybg, w_pa_b, w_o_b, g_post)


def _head_major(w):
    return w.reshape(D_MODEL, 2, HEADS, HEAD_DIM).swapaxes(1, 2).reshape(D_MODEL, HEADS * V_DIM)


def _encoder_layer(x, layer_idx, g_pre, w_in, lambda_q1, lambda_k1, lambda_q2, lambda_k2, subln_g,
                   w_pa, ln_g, ln_b, w_s, b_s, w_pb, w_o, g_post, rel_bias):
    seq = x.shape[1]
    tm, t_out, tq, tk, group, qpack = _tiles(seq)
    assert seq % tk == 0 and tk % tm == 0 and tk % tq == 0 and tq % LANES == 0 and tm % CHUNK == 0
    assert seq % t_out == 0
    lambda_init = 0.8 - 0.6 * math.exp(-0.3 * layer_idx)
    bf16 = jnp.bfloat16
    row = lambda a: a.reshape(1, -1).astype(jnp.float32)

    w_in_b = jnp.concatenate([_head_major(w_in[:, :D_MODEL]), _head_major(w_in[:, D_MODEL:2 * D_MODEL]),
                              w_in[:, 2 * D_MODEL:]], axis=1).astype(bf16)
    bs_b = jnp.broadcast_to(b_s.astype(jnp.float32)[:, :, None], (GROUPS, CHUNK, GROUP_DIM))

    q12, k12, vt, ga, sma, ybg = _project(
        x, row(g_pre), w_in_b, row(ln_g), row(ln_b), w_s.astype(bf16), bs_b, w_pb.astype(bf16), tm, tk)

    strips = _bias_strips(rel_bias, tq, tk)
    lam_vecs = [row(a) for a in (lambda_q1, lambda_k1, lambda_q2, lambda_k2)]
    subg_col = subln_g.reshape(V_DIM, 1).astype(jnp.float32)
    og = _attention(lam_vecs, q12, k12, vt, strips, ga, subg_col, lambda_init, tq, tk, group, qpack)

    return _output(x, og, sma, ybg, w_pa.astype(bf16), w_o.astype(bf16), row(g_post), t_out)


def kernel(x_prompt, x_sample, g_pre, w_in, lambda_q1, lambda_k1, lambda_q2, lambda_k2, subln_g,
           w_pa, ln_g, ln_b, w_s, b_s, w_pb, w_o, g_post, rel_bias):
    y_prompt, y_sample = x_prompt, x_sample
    for l in range(g_pre.shape[0]):
        args = (g_pre[l], w_in[l], lambda_q1[l], lambda_k1[l], lambda_q2[l], lambda_k2[l], subln_g[l],
                w_pa[l], ln_g[l], ln_b[l], w_s[l], b_s[l], w_pb[l], w_o[l], g_post[l], rel_bias)
        y_prompt = _encoder_layer(y_prompt, l, *args)
        y_sample = _encoder_layer(y_sample, l, *args)
    return (y_prompt, y_sample)
```

```python
import functools
import math

import jax
import jax.numpy as jnp
from jax import lax
from jax.experimental import pallas as pl
from jax.experimental.pallas import tpu as pltpu

D_MODEL = 1024
HEADS = 8
HEAD_DIM = 64
V_DIM = 2 * HEAD_DIM
GROUPS = 8
GROUP_DIM = 128
CHUNK = 128
REL_BUCKETS = 32
REL_MAX_DIST = 128
EPS = 1e-6
N_COL_BLOCKS = 9
LOG2E = math.log2(math.e)
LANES = 128
SUBLANES = 8
BF16_SUBLANES = 16
PV_CHUNK = 256
V_ROWS = V_DIM + BF16_SUBLANES
VMEM_LIMIT_BYTES = 56 * 1024 * 1024
NEG_BIG = -1e30
BOUND_SLACK = 1.0 + 2.0 ** -6
MIN_DENOMINATOR = 2.0 ** -64
MAX_DENOMINATOR = 2.0 ** 64
F8 = jnp.float8_e4m3fn
F8_TOP_EXP = 7


def _rel_bucket(rel):
    half = REL_BUCKETS // 2
    max_exact = half // 2
    n = jnp.abs(rel)
    nf = jnp.maximum(n, 1).astype(jnp.float32)
    large = max_exact + (jnp.log(nf / max_exact) / math.log(REL_MAX_DIST / max_exact)
                         * (half - max_exact)).astype(jnp.int32)
    large = jnp.minimum(large, half - 1)
    return jnp.where(rel > 0, half, 0) + jnp.where(n < max_exact, n, large)


def _tiles(seq):
    tm = min(512, seq)
    t_out = min(1024, seq)
    tq = min(256, seq)
    tk = min(1024, seq)
    nk = seq // tk
    group = min(16, nk)
    qpack = 8 if nk <= 4 else 4
    return tm, t_out, tq, tk, group, qpack


def _bias_strip_kernel(bucket_ref, table_ref, out_ref):
    h = pl.program_id(0)
    bucket = bucket_ref[...]
    acc = jnp.zeros(bucket.shape, jnp.float32)
    for b in range(REL_BUCKETS):
        acc = jnp.where(bucket == b, table_ref[b, h], acc)
    out_ref[0] = acc * LOG2E


def _bias_strips(rel_bias, tq, tk):
    n_rows = 2 * tk + 3 * tq
    rows_per_step = tq
    key = jnp.arange(n_rows, dtype=jnp.int32).reshape(n_rows, 1)
    qry = jnp.arange(tq, dtype=jnp.int32).reshape(1, tq)
    bucket = _rel_bucket(key - qry - tk - tq).astype(jnp.int32)
    return pl.pallas_call(
        _bias_strip_kernel,
        grid=(HEADS, n_rows // rows_per_step),
        in_specs=[
            pl.BlockSpec((rows_per_step, tq), lambda h, c: (c, 0)),
            pl.BlockSpec(memory_space=pltpu.SMEM),
        ],
        out_specs=pl.BlockSpec((1, rows_per_step, tq), lambda h, c: (h, c, 0)),
        out_shape=jax.ShapeDtypeStruct((HEADS, n_rows, tq), jnp.float32),
        name="bias_strips",
    )(bucket, rel_bias.astype(jnp.float32))


def _silu(x):
    return x * jax.nn.sigmoid(x)


def _proj_kernel(x_ref, gpre_ref, win_ref, lng_ref, lnb_ref, ws_ref, bs_ref, wpb_ref,
                 q_ref, k_ref, vt_ref, ga_ref, sma_ref, ybg_ref, t_scr):
    tm = x_ref.shape[1]
    x = x_ref[0]
    h = x * lax.rsqrt(jnp.mean(x * x, axis=-1, keepdims=True) + EPS) * gpre_ref[...]
    h = h.astype(jnp.bfloat16)

    def zcol(c):
        return jnp.dot(h, win_ref[:, c * D_MODEL:(c + 1) * D_MODEL],
                       preferred_element_type=jnp.float32)

    def store_heads(ref, z):
        zb = z.astype(jnp.bfloat16)
        for hd in range(HEADS):
            ref[0, hd] = zb[:, hd * V_DIM:(hd + 1) * V_DIM]

    store_heads(q_ref, zcol(0) * (LOG2E * HEAD_DIM ** -0.5))
    store_heads(k_ref, zcol(1))
    zv = zcol(2)
    ones = jnp.ones((BF16_SUBLANES, tm), jnp.bfloat16)
    for hd in range(HEADS):
        vt_ref[0, hd, 0, :V_DIM, :] = zv[:, hd * V_DIM:(hd + 1) * V_DIM].T.astype(jnp.bfloat16)
        vt_ref[0, hd, 0, V_DIM:, :] = ones
    ga_ref[0] = _silu(zcol(3)).astype(jnp.bfloat16)

    vb = zcol(5)
    mu = jnp.mean(vb, axis=-1, keepdims=True)
    vc = vb - mu
    vn = vc * lax.rsqrt(jnp.mean(vc * vc, axis=-1, keepdims=True) + EPS)
    vn = (vn * lng_ref[...] + lnb_ref[...]).astype(jnp.bfloat16)
    ug = zcol(4) * _silu(zcol(6))
    for c in range(tm // CHUNK):
        rows = slice(c * CHUNK, (c + 1) * CHUNK)
        for g in range(GROUPS):
            cols = slice(g * GROUP_DIM, (g + 1) * GROUP_DIM)
            s = jnp.dot(ws_ref[g], vn[rows, cols], preferred_element_type=jnp.float32) + bs_ref[g]
            t_scr[rows, cols] = (ug[rows, cols] * s).astype(jnp.bfloat16)
    y_b = jnp.dot(t_scr[...], wpb_ref[...], preferred_element_type=jnp.float32)

    sma_ref[0] = jax.nn.sigmoid(zcol(7)).astype(jnp.bfloat16)
    ybg_ref[0] = (jax.nn.sigmoid(zcol(8)) * y_b).astype(jnp.bfloat16)


def _resident(shape):
    return pl.BlockSpec(shape, lambda *_: (0,) * len(shape), pipeline_mode=pl.Buffered(1))


def _project(x, g_pre, w_in_b, ln_g, ln_b, w_s_b, bs_b, w_pb_b, tm, tk):
    batch, seq, _ = x.shape
    per_k = tk // tm
    head_shape = jax.ShapeDtypeStruct((batch, HEADS, seq, V_DIM), jnp.bfloat16)
    head_spec = pl.BlockSpec((1, HEADS, tm, V_DIM), lambda b, i: (b, 0, i, 0))
    tok_spec = pl.BlockSpec((1, tm, D_MODEL), lambda b, i: (b, i, 0))
    return pl.pallas_call(
        _proj_kernel,
        grid=(batch, seq // tm),
        in_specs=[
            tok_spec,
            _resident((1, D_MODEL)),
            _resident((D_MODEL, N_COL_BLOCKS * D_MODEL)),
            _resident((1, D_MODEL)),
            _resident((1, D_MODEL)),
            _resident((GROUPS, CHUNK, CHUNK)),
            _resident((GROUPS, CHUNK, GROUP_DIM)),
            _resident((D_MODEL, D_MODEL)),
        ],
        out_specs=[
            head_spec, head_spec,
            pl.BlockSpec((1, HEADS, 1, V_ROWS, tm), lambda b, i: (b, 0, i // per_k, 0, i % per_k)),
            tok_spec, tok_spec, tok_spec],
        out_shape=[
            head_shape, head_shape,
            jax.ShapeDtypeStruct((batch, HEADS, seq // tk, V_ROWS, tk), jnp.bfloat16),
            jax.ShapeDtypeStruct((batch, seq, D_MODEL), jnp.bfloat16),
            jax.ShapeDtypeStruct((batch, seq, D_MODEL), jnp.bfloat16),
            jax.ShapeDtypeStruct((batch, seq, D_MODEL), jnp.bfloat16),
        ],
        scratch_shapes=[pltpu.VMEM((tm, D_MODEL), jnp.bfloat16)],
        compiler_params=pltpu.CompilerParams(
            dimension_semantics=("parallel", "parallel"), vmem_limit_bytes=VMEM_LIMIT_BYTES),
        name="proj_gmlp",
    )(x, g_pre, w_in_b, ln_g, ln_b, w_s_b, bs_b, w_pb_b)


def _scores(lhs, rhs):
    return lax.dot_general(lhs, rhs, (((1,), (1,)), ((), ())), preferred_element_type=jnp.float32)


def _pow2_scales(max_abs):
    bits = lax.bitcast_convert_type(max_abs, jnp.int32)
    e = jnp.clip((bits >> 23) & 0xFF, 16, 250)
    up = lax.bitcast_convert_type((254 + F8_TOP_EXP - e) << 23, jnp.float32)
    down = lax.bitcast_convert_type((e - F8_TOP_EXP) << 23, jnp.float32)
    return up, down


def _split8(x):
    hi = x.astype(F8).astype(jnp.float32)
    lo = (x - hi).astype(F8).astype(jnp.float32)
    return hi, lo


def _map0_pair(a, b, lane):
    return jnp.where(lane < HEAD_DIM, a, pltpu.roll(b, HEAD_DIM, 1))


def _attn_kernel(lq1_ref, lk1_ref, lq2_ref, lk2_ref, q_ref, k_ref, vt_ref, strip_ref,
                 ga_ref, subg_ref, o_ref, acc_scr, info_scr, k8_scr, s_scr, m_scr,
                 *, tq, tk, group, qpack, lambda_init):
    step = pl.program_id(2)
    nk = vt_ref.shape[2]
    ratio = tk // tq

    lane = lax.broadcasted_iota(jnp.int32, (tq, V_DIM), 1)

    def score_maps(j):
        q = q_ref[0, 0, pl.ds(pl.multiple_of(j * tq, tq), tq), :]
        zero = jnp.zeros_like(q)
        return q, (jnp.where(lane < HEAD_DIM, q, zero), jnp.where(lane >= HEAD_DIM, q, zero))

    sel_row = lax.broadcasted_iota(jnp.int32, (BF16_SUBLANES, V_DIM), 0)
    sel_lane = lax.broadcasted_iota(jnp.int32, (BF16_SUBLANES, V_DIM), 1)
    half_sel = jnp.where(sel_row == sel_lane // HEAD_DIM, 1.0, 0.0).astype(jnp.bfloat16)

    def half_norms_sq(x):
        xf = x.astype(jnp.float32)
        return _scores(half_sel, (xf * xf).astype(jnp.bfloat16))

    @pl.when(step == 0)
    def _():
        def key_stats(kt, carry):
            best_sq, best_abs = carry
            kblk = k_ref[0, 0, pl.ds(pl.multiple_of(kt * tk, tk), tk), :]
            kabs = jnp.max(jnp.abs(kblk.astype(jnp.float32)), axis=0, keepdims=True)
            return jnp.maximum(best_sq, half_norms_sq(kblk)), jnp.maximum(best_abs, kabs)
        best_sq, best_abs = lax.fori_loop(
            0, nk, key_stats,
            (jnp.zeros((BF16_SUBLANES, tk), jnp.float32), jnp.zeros((1, V_DIM), jnp.float32)))
        kmax_sq = jnp.max(best_sq, axis=1, keepdims=True)
        bias_max = jnp.max(jnp.max(strip_ref[0], axis=0, keepdims=True), axis=1, keepdims=True)
        k_up, k_down = _pow2_scales(
            jnp.max(jnp.where(lane[:1] < HEAD_DIM, best_abs, 0.0), axis=1, keepdims=True))
        info_row = lax.broadcasted_iota(jnp.int32, info_scr.shape, 0)
        info = jnp.broadcast_to(kmax_sq, info_scr.shape)
        for row_idx, val in ((2, bias_max), (3, k_down)):
            info = jnp.where(info_row == row_idx, val, info)
        info_scr[...] = info

        def key_to_f8(kt, carry):
            k0 = pl.multiple_of(kt * tk, tk)
            hi, lo = _split8(k_ref[0, 0, pl.ds(k0, tk), :].astype(jnp.float32) * k_up)
            pair = _map0_pair(hi, lo, lax.broadcasted_iota(jnp.int32, hi.shape, 1))
            k8_scr[pl.ds(k0, tk), :] = jnp.concatenate([pair, pair], axis=1).astype(F8)
            return carry
        lax.fori_loop(0, nk, key_to_f8, 0)

    def strip_row(j, kt):
        qt = step * qpack + j
        return pl.multiple_of(jnp.clip((kt * ratio - qt + ratio + 1) * tq, 0, tk + 3 * tq), tq)

    lam = (jnp.exp(jnp.sum(lq1_ref[...] * lk1_ref[...], axis=-1, keepdims=True))
           - jnp.exp(jnp.sum(lq2_ref[...] * lk2_ref[...], axis=-1, keepdims=True)) + lambda_init)

    def finalize(j):
        rows = pl.ds(pl.multiple_of(j * tq, tq), tq)
        inv1 = 1.0 / acc_scr[j, 0, V_DIM:V_DIM + 1, :]
        inv2 = lam / acc_scr[j, 1, V_DIM:V_DIM + 1, :]
        o = acc_scr[j, 0, :V_DIM, :] * inv1 - acc_scr[j, 1, :V_DIM, :] * inv2
        o = o * lax.rsqrt(jnp.mean(o * o, axis=0, keepdims=True) + EPS) * subg_ref[...]
        o = o * (1.0 - lambda_init)
        o_ref[0, rows, :] = (o.T * ga_ref[0, rows, :].astype(jnp.float32)).astype(jnp.bfloat16)

    def bound_pass(j):
        q, qmaps = score_maps(j)
        q_sq = half_norms_sq(q)
        bias_max = info_scr[2:3, 0:1]
        ref = [jnp.sqrt(q_sq[mp:mp + 1, :] * info_scr[mp:mp + 1, 0:1]) * BOUND_SLACK + bias_max
               for mp in range(2)]
        qf = q.astype(jnp.float32)
        q_up, q_down = _pow2_scales(jnp.max(jnp.max(jnp.where(lane < HEAD_DIM, jnp.abs(qf), 0.0),
                                                    axis=0, keepdims=True), axis=1, keepdims=True))
        q_hi, q_lo = _split8(qf * q_up)
        q8 = jnp.concatenate([_map0_pair(q_hi, q_hi, lane), _map0_pair(q_lo, q_lo, lane)], axis=1).astype(F8)
        unscale = q_down * info_scr[3:4, 0:1]

        def bound_tile(kt, first, far_bias=None):
            k0 = pl.multiple_of(kt * tk, tk)
            ss = [_scores(k8_scr[pl.ds(k0, tk), :], q8) * unscale,
                  _scores(k_ref[0, 0, pl.ds(k0, tk), :], qmaps[1])]
            if far_bias is None:
                r0 = strip_row(j, kt)
            else:
                shift = [far_bias - ref[mp] for mp in range(2)]
            pv = [None, None]
            for c in range(tk // PV_CHUNK):
                keys = slice(c * PV_CHUNK, (c + 1) * PV_CHUNK)
                if far_bias is None:
                    bias = strip_ref[0, pl.ds(r0 + c * PV_CHUNK, PV_CHUNK), :]
                for mp in range(2):
                    x = ss[mp][keys] + shift[mp] if far_bias is not None else ss[mp][keys] + bias - ref[mp]
                    p = jnp.exp2(x).astype(jnp.bfloat16)
                    part = jnp.dot(vt_ref[0, 0, kt, :, keys], p, preferred_element_type=jnp.float32)
                    pv[mp] = part if pv[mp] is None else pv[mp] + part
            for mp in range(2):
                acc_scr[j, mp] = pv[mp] if first else acc_scr[j, mp] + pv[mp]

        if nk == group:
            own = (step * qpack + j) // ratio
            n_rows = strip_ref.shape[1]
            far_left = strip_ref[0, 0:1, 0:1]
            far_right = strip_ref[0, n_rows - 1:n_rows, 0:1]
            for r in range(nk):
                wrapped = own + r >= nk
                kt = jnp.where(wrapped, own + r - nk, own + r)
                if 2 <= r <= nk - 2:
                    bound_tile(kt, r == 0, jnp.where(wrapped, far_left, far_right))
                else:
                    bound_tile(kt, r == 0)
        else:
            acc_scr[j] = jnp.zeros(acc_scr.shape[1:], jnp.float32)

            def bound_tiles(g, carry):
                for t in range(group):
                    bound_tile(g * group + t, False)
                return carry
            lax.fori_loop(0, nk // group, bound_tiles, 0)

    def running_max_pass(j):
        _, qmaps = score_maps(j)
        m_scr[...] = jnp.full(m_scr.shape, NEG_BIG, jnp.float32)
        acc_scr[j] = jnp.zeros(acc_scr.shape[1:], jnp.float32)

        def tile(kt, carry):
            k0 = pl.multiple_of(kt * tk, tk)
            kblk = k_ref[0, 0, pl.ds(k0, tk), :]
            bias = strip_ref[0, pl.ds(strip_row(j, kt), tk), :]
            for mp in range(2):
                s = _scores(kblk, qmaps[mp]) + bias
                s_scr[...] = s
                m_old = m_scr[mp]
                m_new = jnp.maximum(m_old, jnp.max(s, axis=0, keepdims=True))
                m_scr[mp] = m_new
                p = jnp.exp2(s_scr[...] - m_new).astype(jnp.bfloat16)
                acc_scr[j, mp] = jnp.exp2(m_old - m_new) * acc_scr[j, mp] + jnp.dot(
                    vt_ref[0, 0, kt], p, preferred_element_type=jnp.float32)
            return carry
        lax.fori_loop(0, nk, tile, 0)

    untrusted = 0.0
    for j in range(qpack):
        bound_pass(j)
        denom = acc_scr[j, :, V_DIM:V_DIM + 1, :]
        trusted = (denom >= MIN_DENOMINATOR) & (denom <= MAX_DENOMINATOR)
        untrusted = untrusted + jnp.sum(jnp.where(trusted, 0.0, 1.0))
        finalize(j)

    @pl.when(untrusted > 0.0)
    def _():
        def redo(j, carry):
            running_max_pass(j)
            finalize(j)
            return carry
        lax.fori_loop(0, qpack, redo, 0)


def _attention(lam_vecs, q12, k12, vt, strips, ga, subln_g, lambda_init, tq, tk, group, qpack):
    batch, _, seq, _ = q12.shape
    nk = seq // tk
    tqs = tq * qpack
    assert nk % group == 0 and seq % tqs == 0
    vec = pl.BlockSpec((1, HEAD_DIM), lambda b, h, i: (0, 0))
    return pl.pallas_call(
        functools.partial(_attn_kernel, tq=tq, tk=tk, group=group, qpack=qpack, lambda_init=lambda_init),
        grid=(batch, HEADS, seq // tqs),
        in_specs=[
            vec, vec, vec, vec,
            pl.BlockSpec((1, 1, tqs, V_DIM), lambda b, h, i: (b, h, i, 0)),
            pl.BlockSpec((1, 1, seq, V_DIM), lambda b, h, i: (b, h, 0, 0)),
            pl.BlockSpec((1, 1, nk, V_ROWS, tk), lambda b, h, i: (b, h, 0, 0, 0)),
            pl.BlockSpec((1, strips.shape[1], tq), lambda b, h, i: (h, 0, 0)),
            pl.BlockSpec((1, tqs, V_DIM), lambda b, h, i: (b, i, h)),
            pl.BlockSpec((V_DIM, 1), lambda b, h, i: (0, 0)),
        ],
        out_specs=pl.BlockSpec((1, tqs, V_DIM), lambda b, h, i: (b, i, h)),
        out_shape=jax.ShapeDtypeStruct((batch, seq, HEADS * V_DIM), jnp.bfloat16),
        scratch_shapes=[
            pltpu.VMEM((qpack, 2, V_ROWS, tq), jnp.float32),
            pltpu.VMEM((BF16_SUBLANES, LANES), jnp.float32),
            pltpu.VMEM((seq, 2 * V_DIM), F8),
            pltpu.VMEM((tk, tq), jnp.float32),
            pltpu.VMEM((2, 1, tq), jnp.float32),
        ],
        compiler_params=pltpu.CompilerParams(
            dimension_semantics=("parallel", "parallel", "arbitrary"),
            vmem_limit_bytes=VMEM_LIMIT_BYTES),
        name="diff_attention",
    )(*lam_vecs, q12, k12, vt, strips, ga, subln_g)


def _out_kernel(x_ref, og_ref, sma_ref, ybg_ref, wpa_ref, wo_ref, gpost_ref, y_ref):
    y_a = jnp.dot(og_ref[0], wpa_ref[...], preferred_element_type=jnp.float32)
    merged = sma_ref[0].astype(jnp.float32) * y_a + ybg_ref[0].astype(jnp.float32)
    out = jnp.dot(merged.astype(jnp.bfloat16), wo_ref[...], preferred_element_type=jnp.float32)
    normed = out * lax.rsqrt(jnp.mean(out * out, axis=-1, keepdims=True) + EPS) * gpost_ref[...]
    y_ref[0] = x_ref[0] + normed


def _output(x, og, sma, ybg, w_pa_b, w_o_b, g_post, tm):
    batch, seq, _ = x.shape
    tok_spec = pl.BlockSpec((1, tm, D_MODEL), lambda b, i: (b, i, 0))
    return pl.pallas_call(
        _out_kernel,
        grid=(batch, seq // tm),
        in_specs=[tok_spec, tok_spec, tok_spec, tok_spec,
                  _resident((D_MODEL, D_MODEL)), _resident((D_MODEL, D_MODEL)),
                  _resident((1, D_MODEL))],
        out_specs=tok_spec,
        out_shape=jax.ShapeDtypeStruct(x.shape, x.dtype),
        compiler_params=pltpu.CompilerParams(
            dimension_semantics=("parallel", "parallel"), vmem_limit_bytes=VMEM_LIMIT_BYTES),
        name="merge_out",
    )(x, og, sma, ybg, w_pa_b, w_o_b, g_post)


def _head_major(w):
    return w.reshape(D_MODEL, 2, HEADS, HEAD_DIM).swapaxes(1, 2).reshape(D_MODEL, HEADS * V_DIM)


def _encoder_layer(x, layer_idx, g_pre, w_in, lambda_q1, lambda_k1, lambda_q2, lambda_k2, subln_g,
                   w_pa, ln_g, ln_b, w_s, b_s, w_pb, w_o, g_post, rel_bias):
    seq = x.shape[1]
    tm, t_out, tq, tk, group, qpack = _tiles(seq)
    assert seq % tk == 0 and tk % tm == 0 and tk % tq == 0 and tq % LANES == 0 and tm % CHUNK == 0
    assert seq % t_out == 0
    lambda_init = 0.8 - 0.6 * math.exp(-0.3 * layer_idx)
    bf16 = jnp.bfloat16
    row = lambda a: a.reshape(1, -1).astype(jnp.float32)

    w_in_b = jnp.concatenate([_head_major(w_in[:, :D_MODEL]), _head_major(w_in[:, D_MODEL:2 * D_MODEL]),
                              w_in[:, 2 * D_MODEL:]], axis=1).astype(bf16)
    bs_b = jnp.broadcast_to(b_s.astype(jnp.float32)[:, :, None], (GROUPS, CHUNK, GROUP_DIM))

    q12, k12, vt, ga, sma, ybg = _project(
        x, row(g_pre), w_in_b, row(ln_g), row(ln_b), w_s.astype(bf16), bs_b, w_pb.astype(bf16), tm, tk)

    strips = _bias_strips(rel_bias, tq, tk)
    lam_vecs = [row(a) for a in (lambda_q1, lambda_k1, lambda_q2, lambda_k2)]
    subg_col = subln_g.reshape(V_DIM, 1).astype(jnp.float32)
    og = _attention(lam_vecs, q12, k12, vt, strips, ga, subg_col, lambda_init, tq, tk, group, qpack)

    return _output(x, og, sma, ybg, w_pa.astype(bf16), w_o.astype(bf16), row(g_post), t_out)


def kernel(x_prompt, x_sample, g_pre, w_in, lambda_q1, lambda_k1, lambda_q2, lambda_k2, subln_g,
           w_pa, ln_g, ln_b, w_s, b_s, w_pb, w_o, g_post, rel_bias):
    y_prompt, y_sample = x_prompt, x_sample
    for l in range(g_pre.shape[0]):
        args = (g_pre[l], w_in[l], lambda_q1[l], lambda_k1[l], lambda_q2[l], lambda_k2[l], subln_g[l],
                w_pa[l], ln_g[l], ln_b[l], w_s[l], b_s[l], w_pb[l], w_o[l], g_post[l], rel_bias)
        y_prompt = _encoder_layer(y_prompt, l, *args)
        y_sample = _encoder_layer(y_sample, l, *args)
    return (y_prompt, y_sample)
```

```python
import functools
import math

import jax
import jax.numpy as jnp
from jax import lax
from jax.experimental import pallas as pl
from jax.experimental.pallas import tpu as pltpu

D_MODEL = 1024
HEADS = 8
HEAD_DIM = 64
V_DIM = 2 * HEAD_DIM
GROUPS = 8
GROUP_DIM = 128
CHUNK = 128
REL_BUCKETS = 32
REL_MAX_DIST = 128
EPS = 1e-6
N_COL_BLOCKS = 9
LOG2E = math.log2(math.e)
LANES = 128
SUBLANES = 8
BF16_SUBLANES = 16
PV_CHUNK = 256
V_ROWS = V_DIM + BF16_SUBLANES
VMEM_LIMIT_BYTES = 56 * 1024 * 1024
NEG_BIG = -1e30
BOUND_SLACK = 1.0 + 2.0 ** -6
MIN_DENOMINATOR = 2.0 ** -64
MAX_DENOMINATOR = 2.0 ** 64


def _rel_bucket(rel):
    half = REL_BUCKETS // 2
    max_exact = half // 2
    n = jnp.abs(rel)
    nf = jnp.maximum(n, 1).astype(jnp.float32)
    large = max_exact + (jnp.log(nf / max_exact) / math.log(REL_MAX_DIST / max_exact)
                         * (half - max_exact)).astype(jnp.int32)
    large = jnp.minimum(large, half - 1)
    return jnp.where(rel > 0, half, 0) + jnp.where(n < max_exact, n, large)


def _tiles(seq):
    tm = min(512, seq)
    t_out = min(1024, seq)
    tq = min(256, seq)
    tk = min(1024, seq)
    nk = seq // tk
    group = min(16, nk)
    qpack = 8 if nk <= 4 else 4
    return tm, t_out, tq, tk, group, qpack


def _bias_strip_kernel(bucket_ref, table_ref, out_ref):
    h = pl.program_id(0)
    bucket = bucket_ref[...]
    acc = jnp.zeros(bucket.shape, jnp.float32)
    for b in range(REL_BUCKETS):
        acc = jnp.where(bucket == b, table_ref[b, h], acc)
    out_ref[0] = acc * LOG2E


def _bias_strips(rel_bias, tq, tk):
    n_rows = 2 * tk + 3 * tq
    rows_per_step = tq
    key = jnp.arange(n_rows, dtype=jnp.int32).reshape(n_rows, 1)
    qry = jnp.arange(tq, dtype=jnp.int32).reshape(1, tq)
    bucket = _rel_bucket(key - qry - tk - tq).astype(jnp.int32)
    return pl.pallas_call(
        _bias_strip_kernel,
        grid=(HEADS, n_rows // rows_per_step),
        in_specs=[
            pl.BlockSpec((rows_per_step, tq), lambda h, c: (c, 0)),
            pl.BlockSpec(memory_space=pltpu.SMEM),
        ],
        out_specs=pl.BlockSpec((1, rows_per_step, tq), lambda h, c: (h, c, 0)),
        out_shape=jax.ShapeDtypeStruct((HEADS, n_rows, tq), jnp.float32),
        name="bias_strips",
    )(bucket, rel_bias.astype(jnp.float32))


def _silu(x):
    return x * jax.nn.sigmoid(x)


def _proj_kernel(x_ref, gpre_ref, win_ref, lng_ref, lnb_ref, ws_ref, bs_ref, wpb_ref, gsel_ref,
                 q_ref, k_ref, vt_ref, ga_ref, sma_ref, ybg_ref, kmax_ref, t_scr):
    tm = x_ref.shape[1]
    x = x_ref[0]
    h = x * lax.rsqrt(jnp.mean(x * x, axis=-1, keepdims=True) + EPS) * gpre_ref[...]
    h = h.astype(jnp.bfloat16)

    def zcol(c):
        return jnp.dot(h, win_ref[:, c * D_MODEL:(c + 1) * D_MODEL],
                       preferred_element_type=jnp.float32)

    def store_heads(ref, z):
        zb = z.astype(jnp.bfloat16)
        for hd in range(HEADS):
            ref[0, hd] = zb[:, hd * V_DIM:(hd + 1) * V_DIM]

    store_heads(q_ref, zcol(0) * (LOG2E * HEAD_DIM ** -0.5))
    zk = zcol(1)
    store_heads(k_ref, zk)
    kb = zk.astype(jnp.bfloat16).astype(jnp.float32)
    norms_sq = jnp.dot((kb * kb).astype(jnp.bfloat16), gsel_ref[...], preferred_element_type=jnp.float32)
    tile_max = jnp.max(norms_sq, axis=0, keepdims=True)
    @pl.when(pl.program_id(1) == 0)
    def _():
        kmax_ref[0] = tile_max

    @pl.when(pl.program_id(1) > 0)
    def _():
        kmax_ref[0] = jnp.maximum(kmax_ref[0], tile_max)
    zv = zcol(2)
    ones = jnp.ones((BF16_SUBLANES, tm), jnp.bfloat16)
    for hd in range(HEADS):
        vt_ref[0, hd, 0, :V_DIM, :] = zv[:, hd * V_DIM:(hd + 1) * V_DIM].T.astype(jnp.bfloat16)
        vt_ref[0, hd, 0, V_DIM:, :] = ones
    ga_ref[0] = _silu(zcol(3)).astype(jnp.bfloat16)

    vb = zcol(5)
    mu = jnp.mean(vb, axis=-1, keepdims=True)
    vc = vb - mu
    vn = vc * lax.rsqrt(jnp.mean(vc * vc, axis=-1, keepdims=True) + EPS)
    vn = (vn * lng_ref[...] + lnb_ref[...]).astype(jnp.bfloat16)
    ug = zcol(4) * _silu(zcol(6))
    for c in range(tm // CHUNK):
        rows = slice(c * CHUNK, (c + 1) * CHUNK)
        for g in range(GROUPS):
            cols = slice(g * GROUP_DIM, (g + 1) * GROUP_DIM)
            s = jnp.dot(ws_ref[g], vn[rows, cols], preferred_element_type=jnp.float32) + bs_ref[g]
            t_scr[rows, cols] = (ug[rows, cols] * s).astype(jnp.bfloat16)
    y_b = jnp.dot(t_scr[...], wpb_ref[...], preferred_element_type=jnp.float32)

    sma_ref[0] = jax.nn.sigmoid(zcol(7)).astype(jnp.bfloat16)
    ybg_ref[0] = (jax.nn.sigmoid(zcol(8)) * y_b).astype(jnp.bfloat16)


def _resident(shape):
    return pl.BlockSpec(shape, lambda *_: (0,) * len(shape), pipeline_mode=pl.Buffered(1))


def _project(x, g_pre, w_in_b, ln_g, ln_b, w_s_b, bs_b, w_pb_b, tm, tk):
    batch, seq, _ = x.shape
    per_k = tk // tm
    head_shape = jax.ShapeDtypeStruct((batch, HEADS, seq, V_DIM), jnp.bfloat16)
    head_spec = pl.BlockSpec((1, HEADS, tm, V_DIM), lambda b, i: (b, 0, i, 0))
    tok_spec = pl.BlockSpec((1, tm, D_MODEL), lambda b, i: (b, i, 0))
    col = lax.broadcasted_iota(jnp.int32, (D_MODEL, LANES), 0)
    grp = lax.broadcasted_iota(jnp.int32, (D_MODEL, LANES), 1)
    gsel = jnp.where(col // HEAD_DIM == grp, 1.0, 0.0).astype(jnp.bfloat16)
    return pl.pallas_call(
        _proj_kernel,
        grid=(batch, seq // tm),
        in_specs=[
            tok_spec,
            _resident((1, D_MODEL)),
            _resident((D_MODEL, N_COL_BLOCKS * D_MODEL)),
            _resident((1, D_MODEL)),
            _resident((1, D_MODEL)),
            _resident((GROUPS, CHUNK, CHUNK)),
            _resident((GROUPS, CHUNK, GROUP_DIM)),
            _resident((D_MODEL, D_MODEL)),
            _resident((D_MODEL, LANES)),
        ],
        out_specs=[
            head_spec, head_spec,
            pl.BlockSpec((1, HEADS, 1, V_ROWS, tm), lambda b, i: (b, 0, i // per_k, 0, i % per_k)),
            tok_spec, tok_spec, tok_spec,
            pl.BlockSpec((1, 1, LANES), lambda b, i: (b, 0, 0))],
        out_shape=[
            head_shape, head_shape,
            jax.ShapeDtypeStruct((batch, HEADS, seq // tk, V_ROWS, tk), jnp.bfloat16),
            jax.ShapeDtypeStruct((batch, seq, D_MODEL), jnp.bfloat16),
            jax.ShapeDtypeStruct((batch, seq, D_MODEL), jnp.bfloat16),
            jax.ShapeDtypeStruct((batch, seq, D_MODEL), jnp.bfloat16),
            jax.ShapeDtypeStruct((batch, 1, LANES), jnp.float32),
        ],
        scratch_shapes=[pltpu.VMEM((tm, D_MODEL), jnp.bfloat16)],
        compiler_params=pltpu.CompilerParams(
            dimension_semantics=("parallel", "arbitrary"), vmem_limit_bytes=VMEM_LIMIT_BYTES),
        name="proj_gmlp",
    )(x, g_pre, w_in_b, ln_g, ln_b, w_s_b, bs_b, w_pb_b, gsel)


def _scores(lhs, rhs):
    return lax.dot_general(lhs, rhs, (((1,), (1,)), ((), ())), preferred_element_type=jnp.float32)


def _attn_kernel(table_ref, lq1_ref, lk1_ref, lq2_ref, lk2_ref, q_ref, k_ref, vt_ref, strip_ref,
                 ga_ref, subg_ref, kmax_ref, o_ref, acc_scr, s_scr, m_scr,
                 *, tq, tk, group, qpack, lambda_init):
    hd = pl.program_id(1)
    step = pl.program_id(2)
    nk = vt_ref.shape[2]
    ratio = tk // tq

    lane = lax.broadcasted_iota(jnp.int32, (tq, V_DIM), 1)

    def score_maps(j):
        q = q_ref[0, 0, pl.ds(pl.multiple_of(j * tq, tq), tq), :]
        zero = jnp.zeros_like(q)
        return q, (jnp.where(lane < HEAD_DIM, q, zero), jnp.where(lane >= HEAD_DIM, q, zero))

    sel_row = lax.broadcasted_iota(jnp.int32, (BF16_SUBLANES, V_DIM), 0)
    sel_lane = lax.broadcasted_iota(jnp.int32, (BF16_SUBLANES, V_DIM), 1)
    half_sel = jnp.where(sel_row == sel_lane // HEAD_DIM, 1.0, 0.0).astype(jnp.bfloat16)

    def half_norms_sq(x):
        xf = x.astype(jnp.float32)
        return _scores(half_sel, (xf * xf).astype(jnp.bfloat16))

    bias_max = table_ref[0, hd]
    for b in range(1, REL_BUCKETS):
        bias_max = jnp.maximum(bias_max, table_ref[b, hd])
    bias_max = bias_max * LOG2E
    group_lane = lax.broadcasted_iota(jnp.int32, (1, LANES), 1)
    kmax_sq = [jnp.max(jnp.where(group_lane == 2 * hd + mp, kmax_ref[0], 0.0), axis=1, keepdims=True)
               for mp in range(2)]

    def strip_row(j, kt):
        qt = step * qpack + j
        return pl.multiple_of(jnp.clip((kt * ratio - qt + ratio + 1) * tq, 0, tk + 3 * tq), tq)

    lam = (jnp.exp(jnp.sum(lq1_ref[...] * lk1_ref[...], axis=-1, keepdims=True))
           - jnp.exp(jnp.sum(lq2_ref[...] * lk2_ref[...], axis=-1, keepdims=True)) + lambda_init)

    def finalize(j):
        rows = pl.ds(pl.multiple_of(j * tq, tq), tq)
        inv1 = 1.0 / acc_scr[j, 0, V_DIM:V_DIM + 1, :]
        inv2 = lam / acc_scr[j, 1, V_DIM:V_DIM + 1, :]
        o = acc_scr[j, 0, :V_DIM, :] * inv1 - acc_scr[j, 1, :V_DIM, :] * inv2
        o = o * lax.rsqrt(jnp.mean(o * o, axis=0, keepdims=True) + EPS) * subg_ref[...]
        o = o * (1.0 - lambda_init)
        o_ref[0, rows, :] = (o.T * ga_ref[0, rows, :].astype(jnp.float32)).astype(jnp.bfloat16)

    def bound_pass(j):
        q, qmaps = score_maps(j)
        q_sq = half_norms_sq(q)
        ref = [jnp.sqrt(q_sq[mp:mp + 1, :] * kmax_sq[mp]) * BOUND_SLACK + bias_max
               for mp in range(2)]

        def bound_tile(kt, first, far_bias=None):
            k0 = pl.multiple_of(kt * tk, tk)
            kblk = k_ref[0, 0, pl.ds(k0, tk), :]
            ss = [_scores(kblk, qmaps[mp]) for mp in range(2)]
            if far_bias is None:
                r0 = strip_row(j, kt)
            else:
                shift = [far_bias - ref[mp] for mp in range(2)]
            pv = [None, None]
            for c in range(tk // PV_CHUNK):
                keys = slice(c * PV_CHUNK, (c + 1) * PV_CHUNK)
                if far_bias is None:
                    bias = strip_ref[0, pl.ds(r0 + c * PV_CHUNK, PV_CHUNK), :]
                for mp in range(2):
                    x = ss[mp][keys] + shift[mp] if far_bias is not None else ss[mp][keys] + bias - ref[mp]
                    p = jnp.exp2(x).astype(jnp.bfloat16)
                    part = jnp.dot(vt_ref[0, 0, kt, :, keys], p, preferred_element_type=jnp.float32)
                    pv[mp] = part if pv[mp] is None else pv[mp] + part
            for mp in range(2):
                acc_scr[j, mp] = pv[mp] if first else acc_scr[j, mp] + pv[mp]

        if nk == group:
            own = (step * qpack + j) // ratio
            n_rows = strip_ref.shape[1]
            far_left = strip_ref[0, 0:1, 0:1]
            far_right = strip_ref[0, n_rows - 1:n_rows, 0:1]
            for r in range(nk):
                wrapped = own + r >= nk
                kt = jnp.where(wrapped, own + r - nk, own + r)
                if 2 <= r <= nk - 2:
                    bound_tile(kt, r == 0, jnp.where(wrapped, far_left, far_right))
                else:
                    bound_tile(kt, r == 0)
        else:
            acc_scr[j] = jnp.zeros(acc_scr.shape[1:], jnp.float32)

            def bound_tiles(g, carry):
                for t in range(group):
                    bound_tile(g * group + t, False)
                return carry
            lax.fori_loop(0, nk // group, bound_tiles, 0)

    def running_max_pass(j):
        _, qmaps = score_maps(j)
        m_scr[...] = jnp.full(m_scr.shape, NEG_BIG, jnp.float32)
        acc_scr[j] = jnp.zeros(acc_scr.shape[1:], jnp.float32)

        def tile(kt, carry):
            k0 = pl.multiple_of(kt * tk, tk)
            kblk = k_ref[0, 0, pl.ds(k0, tk), :]
            bias = strip_ref[0, pl.ds(strip_row(j, kt), tk), :]
            for mp in range(2):
                s = _scores(kblk, qmaps[mp]) + bias
                s_scr[...] = s
                m_old = m_scr[mp]
                m_new = jnp.maximum(m_old, jnp.max(s, axis=0, keepdims=True))
                m_scr[mp] = m_new
                p = jnp.exp2(s_scr[...] - m_new).astype(jnp.bfloat16)
                acc_scr[j, mp] = jnp.exp2(m_old - m_new) * acc_scr[j, mp] + jnp.dot(
                    vt_ref[0, 0, kt], p, preferred_element_type=jnp.float32)
            return carry
        lax.fori_loop(0, nk, tile, 0)

    untrusted = 0.0
    for j in range(qpack):
        bound_pass(j)
        denom = acc_scr[j, :, V_DIM:V_DIM + 1, :]
        trusted = (denom >= MIN_DENOMINATOR) & (denom <= MAX_DENOMINATOR)
        untrusted = untrusted + jnp.sum(jnp.where(trusted, 0.0, 1.0))
        finalize(j)

    @pl.when(untrusted > 0.0)
    def _():
        def redo(j, carry):
            running_max_pass(j)
            finalize(j)
            return carry
        lax.fori_loop(0, qpack, redo, 0)


def _attention(table, lam_vecs, q12, k12, vt, strips, ga, subln_g, kmax, lambda_init, tq, tk, group, qpack):
    batch, _, seq, _ = q12.shape
    nk = seq // tk
    tqs = tq * qpack
    assert nk % group == 0 and seq % tqs == 0
    vec = pl.BlockSpec((1, HEAD_DIM), lambda b, h, i: (0, 0))
    return pl.pallas_call(
        functools.partial(_attn_kernel, tq=tq, tk=tk, group=group, qpack=qpack, lambda_init=lambda_init),
        grid=(batch, HEADS, seq // tqs),
        in_specs=[
            pl.BlockSpec(memory_space=pltpu.SMEM),
            vec, vec, vec, vec,
            pl.BlockSpec((1, 1, tqs, V_DIM), lambda b, h, i: (b, h, i, 0)),
            pl.BlockSpec((1, 1, seq, V_DIM), lambda b, h, i: (b, h, 0, 0)),
            pl.BlockSpec((1, 1, nk, V_ROWS, tk), lambda b, h, i: (b, h, 0, 0, 0)),
            pl.BlockSpec((1, strips.shape[1], tq), lambda b, h, i: (h, 0, 0)),
            pl.BlockSpec((1, tqs, V_DIM), lambda b, h, i: (b, i, h)),
            pl.BlockSpec((V_DIM, 1), lambda b, h, i: (0, 0)),
            pl.BlockSpec((1, 1, LANES), lambda b, h, i: (b, 0, 0)),
        ],
        out_specs=pl.BlockSpec((1, tqs, V_DIM), lambda b, h, i: (b, i, h)),
        out_shape=jax.ShapeDtypeStruct((batch, seq, HEADS * V_DIM), jnp.bfloat16),
        scratch_shapes=[
            pltpu.VMEM((qpack, 2, V_ROWS, tq), jnp.float32),
            pltpu.VMEM((tk, tq), jnp.float32),
            pltpu.VMEM((2, 1, tq), jnp.float32),
        ],
        compiler_params=pltpu.CompilerParams(
            dimension_semantics=("parallel", "parallel", "arbitrary"),
            vmem_limit_bytes=VMEM_LIMIT_BYTES),
        name="diff_attention",
    )(table, *lam_vecs, q12, k12, vt, strips, ga, subln_g, kmax)


def _out_kernel(x_ref, og_ref, sma_ref, ybg_ref, wpa_ref, wo_ref, gpost_ref, y_ref):
    y_a = jnp.dot(og_ref[0], wpa_ref[...], preferred_element_type=jnp.float32)
    merged = sma_ref[0].astype(jnp.float32) * y_a + ybg_ref[0].astype(jnp.float32)
    out = jnp.dot(merged.astype(jnp.bfloat16), wo_ref[...], preferred_element_type=jnp.float32)
    normed = out * lax.rsqrt(jnp.mean(out * out, axis=-1, keepdims=True) + EPS) * gpost_ref[...]
    y_ref[0] = x_ref[0] + normed


def _output(x, og, sma, ybg, w_pa_b, w_o_b, g_post, tm):
    batch, seq, _ = x.shape
    tok_spec = pl.BlockSpec((1, tm, D_MODEL), lambda b, i: (b, i, 0))
    return pl.pallas_call(
        _out_kernel,
        grid=(batch, seq // tm),
        in_specs=[tok_spec, tok_spec, tok_spec, tok_spec,
                  _resident((D_MODEL, D_MODEL)), _resident((D_MODEL, D_MODEL)),
                  _resident((1, D_MODEL))],
        out_specs=tok_spec,
        out_shape=jax.ShapeDtypeStruct(x.shape, x.dtype),
        compiler_params=pltpu.CompilerParams(
            dimension_semantics=("parallel", "parallel"), vmem_limit_bytes=VMEM_LIMIT_BYTES),
        name="merge_out",
    )(x, og, sma, ybg, w_pa_b, w_o_b, g_post)


def _head_major(w):
    return w.reshape(D_MODEL, 2, HEADS, HEAD_DIM).swapaxes(1, 2).reshape(D_MODEL, HEADS * V_DIM)


def _encoder_layer(x, layer_idx, g_pre, w_in, lambda_q1, lambda_k1, lambda_q2, lambda_k2, subln_g,
                   w_pa, ln_g, ln_b, w_s, b_s, w_pb, w_o, g_post, rel_bias):
    seq = x.shape[1]
    tm, t_out, tq, tk, group, qpack = _tiles(seq)
    assert seq % tk == 0 and tk % tm == 0 and tk % tq == 0 and tq % LANES == 0 and tm % CHUNK == 0
    assert seq % t_out == 0
    lambda_init = 0.8 - 0.6 * math.exp(-0.3 * layer_idx)
    bf16 = jnp.bfloat16
    row = lambda a: a.reshape(1, -1).astype(jnp.float32)

    w_in_b = jnp.concatenate([_head_major(w_in[:, :D_MODEL]), _head_major(w_in[:, D_MODEL:2 * D_MODEL]),
                              w_in[:, 2 * D_MODEL:]], axis=1).astype(bf16)
    bs_b = jnp.broadcast_to(b_s.astype(jnp.float32)[:, :, None], (GROUPS, CHUNK, GROUP_DIM))

    q12, k12, vt, ga, sma, ybg, kmax = _project(
        x, row(g_pre), w_in_b, row(ln_g), row(ln_b), w_s.astype(bf16), bs_b, w_pb.astype(bf16), tm, tk)

    strips = _bias_strips(rel_bias, tq, tk)
    lam_vecs = [row(a) for a in (lambda_q1, lambda_k1, lambda_q2, lambda_k2)]
    subg_col = subln_g.reshape(V_DIM, 1).astype(jnp.float32)
    og = _attention(rel_bias.astype(jnp.float32), lam_vecs, q12, k12, vt, strips, ga, subg_col, kmax,
                    lambda_init, tq, tk, group, qpack)

    return _output(x, og, sma, ybg, w_pa.astype(bf16), w_o.astype(bf16), row(g_post), t_out)


def kernel(x_prompt, x_sample, g_pre, w_in, lambda_q1, lambda_k1, lambda_q2, lambda_k2, subln_g,
           w_pa, ln_g, ln_b, w_s, b_s, w_pb, w_o, g_post, rel_bias):
    y_prompt, y_sample = x_prompt, x_sample
    for l in range(g_pre.shape[0]):
        args = (g_pre[l], w_in[l], lambda_q1[l], lambda_k1[l], lambda_q2[l], lambda_k2[l], subln_g[l],
                w_pa[l], ln_g[l], ln_b[l], w_s[l], b_s[l], w_pb[l], w_o[l], g_post[l], rel_bias)
        y_prompt = _encoder_layer(y_prompt, l, *args)
        y_sample = _encoder_layer(y_sample, l, *args)
    return (y_prompt, y_sample)
```
